```python
import jax, jax.numpy as jnp
from jax import lax
import numpy as np

D_MODEL = 4096
BATCH = 8
SEQ = 2048
DEPTH = 2

CHUNK = 64
N_MIXERS = 2
HGRN_EXPAND = 128
HGRN_HEADS = D_MODEL // HGRN_EXPAND
HGRN_HEAD_V = D_MODEL // HGRN_HEADS
CONV_WIDTH = 3
D_FF = ((8 * D_MODEL // 3 + 255) // 256) * 256
N_HGRN_LAYERS = (DEPTH + 1) // 2
N_SCONV_LAYERS = DEPTH // 2
N_MOD = 6
EPS = 1e-6

kernel_name = "hybrid_hgrn2_shortconv_adaln_trunk"


def rms_norm(x):
    xf = x.astype(jnp.float32)
    y = xf * lax.rsqrt(jnp.mean(xf * xf, axis=-1, keepdims=True) + EPS)
    return y.astype(x.dtype)


def modulate(h, shift, scale):
    return h * (1 + scale[:, None, :]) + shift[:, None, :]


def causal_dwconv(x, w):
    K = w.shape[-1]
    T = x.shape[1]
    xp = jnp.pad(x, ((0, 0), (K - 1, 0), (0, 0)))
    y = xp[:, 0:T, :] * w[:, 0]
    for k in range(1, K):
        y = y + xp[:, k:k + T, :] * w[:, k]
    return y


def hgrn2_mixer(h, w_in, g_norm, w_out, lower_bound):
    Bsz, T, _ = h.shape
    nc = T // CHUNK
    proj = h @ w_in
    q, f_raw, v, g = jnp.split(proj, 4, axis=-1)
    lb = lower_bound.astype(jnp.float32)
    f = lb + (1.0 - lb) * jax.nn.sigmoid(f_raw.astype(jnp.float32))
    log_f = jnp.log(f)
    k = 1.0 - f

    def to_chunks(z, d):
        return z.astype(jnp.float32).reshape(Bsz, nc, CHUNK, HGRN_HEADS, d).transpose(1, 0, 3, 2, 4)

    qc = to_chunks(q, HGRN_EXPAND)
    kc = to_chunks(k, HGRN_EXPAND)
    lfc = to_chunks(log_f, HGRN_EXPAND)
    vc = to_chunks(v, HGRN_HEAD_V)
    causal = jnp.tril(jnp.ones((CHUNK, CHUNK), dtype=bool))[None, None, :, :, None]

    def step(S, inp):
        q_, k_, v_, lf_ = inp
        b = jnp.cumsum(lf_, axis=2)
        o_inter = jnp.einsum('bhtd,bhde->bhte', q_ * jnp.exp(b), S)
        diff = b[:, :, :, None, :] - b[:, :, None, :, :]
        decay = jnp.exp(jnp.where(causal, diff, -jnp.inf))
        A = jnp.einsum('bhtd,bhsd,bhtsd->bhts', q_, k_, decay)
        o_intra = jnp.einsum('bhts,bhse->bhte', A, v_)
        b_last = b[:, :, -1:, :]
        S_new = jnp.exp(b_last[:, :, 0, :])[..., None] * S + jnp.einsum(
            'bhsd,bhse->bhde', k_ * jnp.exp(b_last - b), v_)
        return S_new, o_inter + o_intra

    S0 = jnp.zeros((Bsz, HGRN_HEADS, HGRN_EXPAND, HGRN_HEAD_V), jnp.float32)
    _, oc = lax.scan(step, S0, (qc, kc, vc, lfc))
    o = oc.transpose(1, 0, 3, 2, 4).reshape(Bsz, T, HGRN_HEADS, HGRN_HEAD_V)
    o = rms_norm(o) * g_norm.astype(jnp.float32).reshape(HGRN_HEADS, HGRN_HEAD_V)
    o = o * jax.nn.silu(g.astype(jnp.float32)).reshape(Bsz, T, HGRN_HEADS, HGRN_HEAD_V)
    return o.reshape(Bsz, T, D_MODEL).astype(h.dtype) @ w_out


def short_conv_mixer(h, w_in, conv_w, w_out):
    Bg, Cg, xv = jnp.split(h @ w_in, 3, axis=-1)
    return (Bg * causal_dwconv(Cg * xv, conv_w)) @ w_out


def conv_ffn(h, w_up, conv_w, conv_b, w_down):
    a, u = jnp.split(h @ w_up, 2, axis=-1)
    a = causal_dwconv(a, conv_w) + conv_b
    return (jax.nn.silu(a) * u) @ w_down


def setup_inputs(seed: int = 0) -> dict:
    key = jax.random.key(seed)
    ks = jax.random.split(key, 20)
    D, F = D_MODEL, D_FF
    nrm = jax.random.normal
    sD = D ** -0.5
    return {
        "x": nrm(ks[0], (BATCH, SEQ, D), jnp.float32),
        "c": nrm(ks[1], (BATCH, D), jnp.float32),
        "w_cond": nrm(ks[2], (D, D), jnp.float32) * sD,
        "b_cond": nrm(ks[3], (D,), jnp.float32) * 0.01,
        "ada_w": nrm(ks[4], (DEPTH, D, N_MOD * D), jnp.float32) * (0.5 * sD),
        "ada_b": nrm(ks[5], (DEPTH, N_MOD * D), jnp.float32) * 0.01,
        "lb_logits": nrm(ks[6], (DEPTH + 1, D), jnp.float32) * 0.5,
        "hgrn_w_in": nrm(ks[7], (N_HGRN_LAYERS, D, 4 * D), jnp.float32) * sD,
        "hgrn_norm": 1.0 + 0.02 * nrm(ks[8], (N_HGRN_LAYERS, D), jnp.float32),
        "hgrn_w_out": nrm(ks[9], (N_HGRN_LAYERS, D, D), jnp.float32) * sD,
        "sconv_w_in": nrm(ks[10], (N_SCONV_LAYERS, D, 3 * D), jnp.float32) * sD,
        "sconv_conv_w": nrm(ks[11], (N_SCONV_LAYERS, D, CONV_WIDTH), jnp.float32) * CONV_WIDTH ** -0.5,
        "sconv_w_out": nrm(ks[12], (N_SCONV_LAYERS, D, D), jnp.float32) * sD,
        "ffn_w_up": nrm(ks[13], (DEPTH, D, 2 * F), jnp.float32) * sD,
        "ffn_conv_w": nrm(ks[14], (DEPTH, F, CONV_WIDTH), jnp.float32) * CONV_WIDTH ** -0.5,
        "ffn_conv_b": nrm(ks[15], (DEPTH, F), jnp.float32) * 0.01,
        "ffn_w_down": nrm(ks[16], (DEPTH, F, D), jnp.float32) * F ** -0.5,
        "final_norm": 1.0 + 0.02 * nrm(ks[17], (D,), jnp.float32),
    }


def reference(x, c, w_cond, b_cond, ada_w, ada_b, lb_logits, hgrn_w_in, hgrn_norm,
              hgrn_w_out, sconv_w_in, sconv_conv_w, sconv_w_out, ffn_w_up, ffn_conv_w,
              ffn_conv_b, ffn_w_down, final_norm):
    c_emb = jax.nn.silu(c @ w_cond + b_cond)
    lbs = jnp.cumsum(jax.nn.softmax(lb_logits.astype(jnp.float32), axis=0), axis=0)
    for i in range(DEPTH):
        mod = c_emb @ ada_w[i] + ada_b[i]
        sh1, sc1, g1, sh2, sc2, g2 = jnp.split(mod, N_MOD, axis=-1)
        hm = modulate(rms_norm(x), sh1, sc1)
        j = i // N_MIXERS
        if i % N_MIXERS == 0:
            y = hgrn2_mixer(hm, hgrn_w_in[j], hgrn_norm[j], hgrn_w_out[j], lbs[i])
        else:
            y = short_conv_mixer(hm, sconv_w_in[j], sconv_conv_w[j], sconv_w_out[j])
        x = x + g1[:, None, :] * y
        hf = modulate(rms_norm(x), sh2, sc2)
        x = x + g2[:, None, :] * conv_ffn(hf, ffn_w_up[i], ffn_conv_w[i], ffn_conv_b[i], ffn_w_down[i])
    return rms_norm(x) * final_norm
```

```python
import functools

import numpy as np
import jax
import jax.numpy as jnp
from jax import lax
from jax.experimental import pallas as pl
from jax.experimental.pallas import tpu as pltpu

EPS = 1e-6
CHUNK = 64
HEAD = 128
N_MOD = 6
F32 = jnp.float32
BF16 = jnp.bfloat16

V7X_VMEM_BYTES = 64 * 1024 * 1024
VMEM_LIMIT = V7X_VMEM_BYTES - 8 * 1024 * 1024
LANE = 128
SUBLANE = 8


def _tile(n, pref, align):
    t = min(pref, n)
    t -= t % align
    while t >= align:
        if n % t == 0:
            return t
        t -= align
    return n


def _params(sem):
    return pltpu.CompilerParams(dimension_semantics=sem, vmem_limit_bytes=VMEM_LIMIT)


def _dot(a, b):
    return jnp.dot(a, b, preferred_element_type=F32)


def _dot_nt(a, b):
    return lax.dot_general(a, b, (((1,), (1,)), ((), ())), preferred_element_type=F32)


def _dot_tn(a, b):
    return lax.dot_general(a, b, (((0,), (0,)), ((), ())), preferred_element_type=F32)


def _silu(x):
    return x * jax.nn.sigmoid(x)


def _cond_kernel(c_ref, w_ref, b_ref, o_ref, *, act):
    acc = _dot(c_ref[...].astype(BF16), w_ref[...].astype(BF16)) + b_ref[...]
    o_ref[...] = _silu(acc) if act else acc


def _cond_matmul(c, w, b, *, act):
    L, K, N = w.shape
    B = c.shape[0]
    bn = _tile(N, 512, LANE)
    return pl.pallas_call(
        functools.partial(_cond_kernel, act=act),
        out_shape=jax.ShapeDtypeStruct((L, B, N), F32),
        grid=(L, N // bn),
        in_specs=[
            pl.BlockSpec((B, K), lambda l, j: (0, 0)),
            pl.BlockSpec((None, K, bn), lambda l, j: (l, 0, j)),
            pl.BlockSpec((None, 1, bn), lambda l, j: (l, 0, j)),
        ],
        out_specs=pl.BlockSpec((None, B, bn), lambda l, j: (l, 0, j)),
        compiler_params=_params(("parallel", "parallel")),
        name="cond_matmul",
    )(c, w, b.reshape(L, 1, N))


def _prenorm_kernel(x_ref, sh_ref, sc_ref, o_ref):
    x = x_ref[...]
    y = x * lax.rsqrt(jnp.mean(x * x, axis=-1, keepdims=True) + EPS)
    o_ref[...] = (y * (1.0 + sc_ref[...]) + sh_ref[...]).astype(o_ref.dtype)


def _prenorm(x3, mod3, k_shift, k_scale):
    B, T, D = x3.shape
    bt = _tile(T, 256, SUBLANE)
    return pl.pallas_call(
        _prenorm_kernel,
        out_shape=jax.ShapeDtypeStruct((B, T, D), BF16),
        grid=(B, T // bt),
        in_specs=[
            pl.BlockSpec((None, bt, D), lambda b, t: (b, t, 0)),
            pl.BlockSpec((None, 1, D), lambda b, t: (b, 0, k_shift)),
            pl.BlockSpec((None, 1, D), lambda b, t: (b, 0, k_scale)),
        ],
        out_specs=pl.BlockSpec((None, bt, D), lambda b, t: (b, t, 0)),
        compiler_params=_params(("parallel", "parallel")),
        name="prenorm",
    )(x3, mod3, mod3)


def _final_norm_kernel(x_ref, w_ref, o_ref):
    x = x_ref[...]
    o_ref[...] = x * lax.rsqrt(jnp.mean(x * x, axis=-1, keepdims=True) + EPS) * w_ref[...]


def _final_norm(x2, w):
    M, D = x2.shape
    bt = _tile(M, 256, SUBLANE)
    return pl.pallas_call(
        _final_norm_kernel,
        out_shape=jax.ShapeDtypeStruct((M, D), F32),
        grid=(M // bt,),
        in_specs=[pl.BlockSpec((bt, D), lambda i: (i, 0)),
                  pl.BlockSpec((1, D), lambda i: (0, 0))],
        out_specs=pl.BlockSpec((bt, D), lambda i: (i, 0)),
        compiler_params=_params(("parallel",)),
        name="final_norm",
    )(x2, w.reshape(1, D))


def _mm_kernel(a_ref, w_ref, o_ref):
    o_ref[...] = _dot(a_ref[...], w_ref[...]).astype(o_ref.dtype)


def _matmul(a, w, out_dtype):
    M, K = a.shape
    N = w.shape[1]
    bm = _tile(M, 1024, SUBLANE)
    bn = _tile(N, 1024, LANE)
    return pl.pallas_call(
        _mm_kernel,
        out_shape=jax.ShapeDtypeStruct((M, N), out_dtype),
        grid=(M // bm, N // bn),
        in_specs=[pl.BlockSpec((bm, K), lambda i, j: (i, 0)),
                  pl.BlockSpec((K, bn), lambda i, j: (0, j))],
        out_specs=pl.BlockSpec((bm, bn), lambda i, j: (i, j)),
        compiler_params=_params(("parallel", "parallel")),
        name="matmul",
    )(a, w)


def _mm_res_kernel(a_ref, w_ref, x_ref, g_ref, o_ref, acc_ref, *, nk):
    k = pl.program_id(2)
    part = _dot(a_ref[...], w_ref[...])
    if nk == 1:
        o_ref[...] = x_ref[...] + g_ref[...] * part
        return

    @pl.when(k == 0)
    def _():
        acc_ref[...] = part

    @pl.when(jnp.logical_and(k > 0, k < nk - 1))
    def _():
        acc_ref[...] += part

    @pl.when(k == nk - 1)
    def _():
        o_ref[...] = x_ref[...] + g_ref[...] * (acc_ref[...] + part)


def _matmul_residual(a, w, x2, mod3, k_gate, seq, *, bk_pref):
    M, K = a.shape
    N = w.shape[1]
    bm = _tile(seq, 1024, SUBLANE)
    bn = _tile(N, 1024, LANE)
    bk = _tile(K, bk_pref, 2 * LANE)
    nk = K // bk
    nb = N // bn
    per_seq = seq // bm
    return pl.pallas_call(
        functools.partial(_mm_res_kernel, nk=nk),
        out_shape=jax.ShapeDtypeStruct((M, N), F32),
        grid=(M // bm, nb, nk),
        in_specs=[
            pl.BlockSpec((bm, bk), lambda i, j, k: (i, k)),
            pl.BlockSpec((bk, bn), lambda i, j, k: (k, j)),
            pl.BlockSpec((bm, bn), lambda i, j, k: (i, j)),
            pl.BlockSpec((None, 1, bn), lambda i, j, k: (i // per_seq, 0, k_gate * nb + j)),
        ],
        out_specs=pl.BlockSpec((bm, bn), lambda i, j, k: (i, j)),
        scratch_shapes=[pltpu.VMEM((bm, bn) if nk > 1 else (SUBLANE, LANE), F32)],
        compiler_params=_params(("parallel", "parallel", "arbitrary")),
        name="matmul_residual",
    )(a, w, x2, mod3)


def _causal_conv3(p, halo, w_ref):
    row = lax.broadcasted_iota(jnp.int32, p.shape, 0)
    n = p.shape[0]
    s1 = pltpu.roll(p, 1, axis=0)
    s2 = pltpu.roll(p, 2, axis=0)
    h7 = halo[SUBLANE - 1:SUBLANE, :]
    h6 = halo[SUBLANE - 2:SUBLANE - 1, :]
    s1 = jnp.where(row == 0, h7, s1)
    s2 = jnp.where(row == 0, h6, jnp.where(row == 1, h7, s2))
    del n
    return w_ref[0:1, :] * s2 + w_ref[1:2, :] * s1 + w_ref[2:3, :] * p


def _load_halo(halo_ref, j, first):
    h = halo_ref[j]
    return jnp.where(first, jnp.zeros_like(h), h)


def _sconv_in_kernel(a_ref, wb_ref, wc_ref, wx_ref, cw_ref, o_ref, halo_ref, *, per_seq):
    i = pl.program_id(0)
    j = pl.program_id(1)
    a = a_ref[...]
    p = _dot(a, wc_ref[...]) * _dot(a, wx_ref[...])
    halo = _load_halo(halo_ref, j, i % per_seq == 0)
    halo_ref[j] = p[p.shape[0] - SUBLANE:, :]
    o_ref[...] = (_dot(a, wb_ref[...]) * _causal_conv3(p, halo, cw_ref)).astype(o_ref.dtype)


def _sconv_in(a, w_in, conv_w3, seq):
    M, D = a.shape
    bm = _tile(seq, 1024, SUBLANE)
    bn = _tile(D, 256, LANE)
    nb = D // bn
    return pl.pallas_call(
        functools.partial(_sconv_in_kernel, per_seq=seq // bm),
        out_shape=jax.ShapeDtypeStruct((M, D), BF16),
        grid=(M // bm, nb),
        in_specs=[
            pl.BlockSpec((bm, D), lambda i, j: (i, 0)),
            pl.BlockSpec((D, bn), lambda i, j: (0, j)),
            pl.BlockSpec((D, bn), lambda i, j: (0, nb + j)),
            pl.BlockSpec((D, bn), lambda i, j: (0, 2 * nb + j)),
            pl.BlockSpec((3, bn), lambda i, j: (0, j)),
        ],
        out_specs=pl.BlockSpec((bm, bn), lambda i, j: (i, j)),
        scratch_shapes=[pltpu.VMEM((nb, SUBLANE, bn), F32)],
        compiler_params=_params(("arbitrary", "arbitrary")),
        name="sconv_in",
    )(a, w_in, w_in, w_in, conv_w3)


def _ffn_up_kernel(a_ref, wa_ref, wu_ref, cw_ref, cb_ref, o_ref, halo_ref, *, per_seq):
    i = pl.program_id(0)
    j = pl.program_id(1)
    a = a_ref[...]
    p = _dot(a, wa_ref[...])
    halo = _load_halo(halo_ref, j, i % per_seq == 0)
    halo_ref[j] = p[p.shape[0] - SUBLANE:, :]
    gate = _silu(_causal_conv3(p, halo, cw_ref) + cb_ref[...])
    o_ref[...] = (gate * _dot(a, wu_ref[...])).astype(o_ref.dtype)


def _ffn_up(a, w_up, conv_w3, conv_b, seq, bn):
    M, D = a.shape
    Fp = w_up.shape[1] // 2
    bm = _tile(seq, 1024, SUBLANE)
    nb = Fp // bn
    return pl.pallas_call(
        functools.partial(_ffn_up_kernel, per_seq=seq // bm),
        out_shape=jax.ShapeDtypeStruct((M, Fp), BF16),
        grid=(M // bm, nb),
        in_specs=[
            pl.BlockSpec((bm, D), lambda i, j: (i, 0)),
            pl.BlockSpec((D, bn), lambda i, j: (0, j)),
            pl.BlockSpec((D, bn), lambda i, j: (0, nb + j)),
            pl.BlockSpec((3, bn), lambda i, j: (0, j)),
            pl.BlockSpec((1, bn), lambda i, j: (0, j)),
        ],
        out_specs=pl.BlockSpec((bm, bn), lambda i, j: (i, j)),
        scratch_shapes=[pltpu.VMEM((nb, SUBLANE, bn), F32)],
        compiler_params=_params(("arbitrary", "arbitrary")),
        name="ffn_up",
    )(a, w_up, w_up, conv_w3, conv_b)


def _range_matrix(C):
    n_lvl = int(np.log2(C))
    u = np.arange(C)[None, :]
    t = np.arange(C)[:, None]
    bands = [u <= t, u > t]
    for l in range(n_lvl):
        h = 1 << l
        m = (t // (2 * h)) * (2 * h) + h
        upper = t >= m
        bands.append(np.where(upper, (u > m) & (u <= t), (u > t) & (u <= m)))
    d = np.concatenate(bands, axis=0).astype(np.float32)
    return np.concatenate([d, d, d], axis=1)


def _hgrn_kernel(q_ref, f_ref, v_ref, g_ref, lbl_ref, gn_ref, dmat_ref, o_ref, st_ref,
                 *, n_chunks, slot):
    C = CHUNK
    W = 2 * HEAD
    n_lvl = int(np.log2(C))

    @pl.when(pl.program_id(2) == 0)
    def _():
        st_ref[...] = jnp.zeros_like(st_ref)

    rows = [lbl_ref[k:k + 1, :] for k in range(lbl_ref.shape[0])]
    mx = functools.reduce(jnp.maximum, rows)
    ex = [jnp.exp(r - mx) for r in rows]
    lb = functools.reduce(jnp.add, ex[:slot + 1]) / functools.reduce(jnp.add, ex)
    gn = gn_ref[...]

    def body(ci, carry):
        r0 = pl.multiple_of(ci * C, C)
        rs = pl.ds(r0, C)
        q = q_ref[rs, :]
        v = v_ref[rs, :]
        f = lb + (1.0 - lb) * jax.nn.sigmoid(f_ref[rs, :])
        k = 1.0 - f
        lg = jnp.log(f)
        hi = lg.astype(BF16)
        r1 = lg - hi.astype(F32)
        mid = r1.astype(BF16)
        lo = (r1 - mid.astype(F32)).astype(BF16)
        args = _dot(dmat_ref[...], jnp.concatenate([hi, mid, lo], axis=0))
        e = jnp.exp(args)

        row = lax.broadcasted_iota(jnp.int32, (C, W), 0)
        ti = lax.broadcasted_iota(jnp.int32, (C, 2 * C), 0)
        si = lax.broadcasted_iota(jnp.int32, (C, 2 * C), 1) & (C - 1)
        xr = ti ^ si
        zc = jnp.zeros((C, HEAD), BF16)

        def by_head(x):
            return jnp.concatenate(
                [jnp.concatenate([x[:, :HEAD], zc], axis=1),
                 jnp.concatenate([zc, x[:, HEAD:]], axis=1)], axis=0)

        st = st_ref[...]
        o = _dot_nt((q * e[0:C]).astype(BF16), st.astype(BF16))
        a = jnp.where(ti == si, _dot_nt(q.astype(BF16), by_head(k.astype(BF16))), 0.0)
        for l in range(n_lvl):
            el = e[(2 + l) * C:(3 + l) * C]
            up = ((row >> l) & 1) == 1
            qe = jnp.where(up, q * el, 0.0).astype(BF16)
            ke = jnp.where(up, 0.0, k * el).astype(BF16)
            a = jnp.where(jnp.logical_and(ti > si, (xr >> l) == 1), _dot_nt(qe, by_head(ke)), a)
        vb = v.astype(BF16)
        o = o + _dot(a.astype(BF16), by_head(vb))

        kh = (k * e[C:2 * C]).astype(BF16)
        upd = _dot_tn(vb, kh)
        same = (lax.broadcasted_iota(jnp.int32, (W, W), 0) // HEAD
                == lax.broadcasted_iota(jnp.int32, (W, W), 1) // HEAD)
        st_ref[...] = st * e[C - 1:C, :] + jnp.where(same, upd, 0.0)

        s2 = o * o
        shi = s2.astype(BF16)
        slo = (s2 - shi.astype(F32)).astype(BF16)
        sums = _dot(jnp.concatenate([shi, slo], axis=0), jnp.where(same, 1.0, 0.0).astype(BF16))
        ms = (sums[:C] + sums[C:]) * (1.0 / HEAD)
        g = g_ref[rs, :]
        o_ref[rs, :] = (o * lax.rsqrt(ms + EPS) * gn * _silu(g)).astype(o_ref.dtype)
        return carry

    lax.fori_loop(0, n_chunks, body, 0)


def _hgrn_recurrence(proj, lb_logits, g_norm, batch, seq, slot):
    M, D4 = proj.shape
    D = D4 // 4
    W = 2 * HEAD
    npair = D // W
    tb = _tile(seq, 1024, CHUNK)
    nt = seq // tb
    dmat = jnp.asarray(_range_matrix(CHUNK), dtype=BF16)

    def col(part):
        return pl.BlockSpec((tb, W), lambda b, p, t: (b * nt + t, part * npair + p))

    return pl.pallas_call(
        functools.partial(_hgrn_kernel, n_chunks=tb // CHUNK, slot=slot),
        out_shape=jax.ShapeDtypeStruct((M, D), BF16),
        grid=(batch, npair, nt),
        in_specs=[col(0), col(1), col(2), col(3),
                  pl.BlockSpec((lb_logits.shape[0], W), lambda b, p, t: (0, p)),
                  pl.BlockSpec((1, W), lambda b, p, t: (0, p)),
                  pl.BlockSpec(dmat.shape, lambda b, p, t: (0, 0))],
        out_specs=pl.BlockSpec((tb, W), lambda b, p, t: (b * nt + t, p)),
        scratch_shapes=[pltpu.VMEM((W, W), F32)],
        compiler_params=_params(("parallel", "parallel", "arbitrary")),
        name="hgrn_recurrence",
    )(proj, proj, proj, proj, lb_logits, g_norm.reshape(1, D), dmat)


def kernel(x, c, w_cond, b_cond, ada_w, ada_b, lb_logits, hgrn_w_in, hgrn_norm, hgrn_w_out,
           sconv_w_in, sconv_conv_w, sconv_w_out, ffn_w_up, ffn_conv_w, ffn_conv_b,
           ffn_w_down, final_norm):
    B, T, D = x.shape
    depth = ada_w.shape[0]
    F = ffn_conv_b.shape[1]
    M = B * T
    ffn_bn = min(512, -(-F // LANE) * LANE)
    Fp = -(-F // ffn_bn) * ffn_bn
    n_mixers = 2

    c_emb = _cond_matmul(c, w_cond[None], b_cond[None], act=True)[0]
    mod = _cond_matmul(c_emb, ada_w, ada_b, act=False)

    x2 = x.reshape(M, D)
    for i in range(depth):
        mod3 = mod[i].reshape(B, 1, N_MOD * D)
        j = i // n_mixers
        hm = _prenorm(x2.reshape(B, T, D), mod3, 0, 1).reshape(M, D)
        if i % n_mixers == 0:
            proj = _matmul(hm, hgrn_w_in[j].astype(BF16), F32)
            y = _hgrn_recurrence(proj, lb_logits, hgrn_norm[j], B, T, slot=i)
            w_out = hgrn_w_out[j]
        else:
            y = _sconv_in(hm, sconv_w_in[j].astype(BF16), sconv_conv_w[j].T, T)
            w_out = sconv_w_out[j]
        x2 = _matmul_residual(y, w_out.astype(BF16), x2, mod3, 2, T, bk_pref=D)

        hf = _prenorm(x2.reshape(B, T, D), mod3, 3, 4).reshape(M, D)
        pad = ((0, 0), (0, Fp - F))
        w_up = jnp.concatenate(
            [jnp.pad(ffn_w_up[i][:, :F].astype(BF16), pad),
             jnp.pad(ffn_w_up[i][:, F:].astype(BF16), pad)], axis=1)
        cw = jnp.pad(ffn_conv_w[i].T, pad)
        cb = jnp.pad(ffn_conv_b[i].reshape(1, F), pad)
        h = _ffn_up(hf, w_up, cw, cb, T, ffn_bn)
        w_down = jnp.pad(ffn_w_down[i].astype(BF16), ((0, Fp - F), (0, 0)))
        x2 = _matmul_residual(h, w_down, x2, mod3, 5, T, bk_pref=Fp // 4)
    return _final_norm(x2, final_norm).reshape(B, T, D)
```

```python
import functools

import numpy as np
import jax
import jax.numpy as jnp
from jax import lax
from jax.experimental import pallas as pl
from jax.experimental.pallas import tpu as pltpu

EPS = 1e-6
CHUNK = 64
HEAD = 128
N_MOD = 6
LOG2E = 1.4426950408889634
F32 = jnp.float32
BF16 = jnp.bfloat16

V7X_VMEM_BYTES = 64 * 1024 * 1024
VMEM_LIMIT = V7X_VMEM_BYTES - 8 * 1024 * 1024
LANE = 128
SUBLANE = 8
FFN_BN = 512


def _tile(n, pref, align):
    t = min(pref, n)
    t -= t % align
    while t >= align:
        if n % t == 0:
            return t
        t -= align
    return n


def _params(sem):
    return pltpu.CompilerParams(dimension_semantics=sem, vmem_limit_bytes=VMEM_LIMIT)


def _dot(a, b):
    return jnp.dot(a, b, preferred_element_type=F32)


def _dot_nt(a, b):
    return lax.dot_general(a, b, (((1,), (1,)), ((), ())), preferred_element_type=F32)


def _dot_tn(a, b):
    return lax.dot_general(a, b, (((0,), (0,)), ((), ())), preferred_element_type=F32)


def _silu(x):
    return x * jax.nn.sigmoid(x)


def _cond_kernel(c_ref, w_ref, b_ref, o_ref, *, act):
    acc = _dot(c_ref[...].astype(BF16), w_ref[...].astype(BF16)) + b_ref[...]
    o_ref[...] = _silu(acc) if act else acc


def _cond_matmul(c, w, b, *, act):
    L, K, N = w.shape
    B = c.shape[0]
    bn = _tile(N, 512, LANE)
    return pl.pallas_call(
        functools.partial(_cond_kernel, act=act),
        out_shape=jax.ShapeDtypeStruct((L, B, N), F32),
        grid=(L, N // bn),
        in_specs=[
            pl.BlockSpec((B, K), lambda l, j: (0, 0)),
            pl.BlockSpec((None, K, bn), lambda l, j: (l, 0, j)),
            pl.BlockSpec((None, 1, bn), lambda l, j: (l, 0, j)),
        ],
        out_specs=pl.BlockSpec((None, B, bn), lambda l, j: (l, 0, j)),
        compiler_params=_params(("parallel", "parallel")),
        name="cond_matmul",
    )(c, w, b.reshape(L, 1, N))


def _prenorm_kernel(x_ref, sh_ref, sc_ref, o_ref):
    x = x_ref[...]
    y = x * lax.rsqrt(jnp.mean(x * x, axis=-1, keepdims=True) + EPS)
    o_ref[...] = (y * (1.0 + sc_ref[...]) + sh_ref[...]).astype(o_ref.dtype)


def _prenorm(x3, mod3, k_shift, k_scale):
    B, T, D = x3.shape
    bt = _tile(T, 256, SUBLANE)
    return pl.pallas_call(
        _prenorm_kernel,
        out_shape=jax.ShapeDtypeStruct((B, T, D), BF16),
        grid=(B, T // bt),
        in_specs=[
            pl.BlockSpec((None, bt, D), lambda b, t: (b, t, 0)),
            pl.BlockSpec((None, 1, D), lambda b, t: (b, 0, k_shift)),
            pl.BlockSpec((None, 1, D), lambda b, t: (b, 0, k_scale)),
        ],
        out_specs=pl.BlockSpec((None, bt, D), lambda b, t: (b, t, 0)),
        compiler_params=_params(("parallel", "parallel")),
        name="prenorm",
    )(x3, mod3, mod3)


def _final_norm_kernel(x_ref, w_ref, o_ref):
    x = x_ref[...]
    o_ref[...] = x * lax.rsqrt(jnp.mean(x * x, axis=-1, keepdims=True) + EPS) * w_ref[...]


def _final_norm(x2, w):
    M, D = x2.shape
    bt = _tile(M, 256, SUBLANE)
    return pl.pallas_call(
        _final_norm_kernel,
        out_shape=jax.ShapeDtypeStruct((M, D), F32),
        grid=(M // bt,),
        in_specs=[pl.BlockSpec((bt, D), lambda i: (i, 0)),
                  pl.BlockSpec((1, D), lambda i: (0, 0))],
        out_specs=pl.BlockSpec((bt, D), lambda i: (i, 0)),
        compiler_params=_params(("parallel",)),
        name="final_norm",
    )(x2, w.reshape(1, D))


def _mm_kernel(a_ref, w_ref, o_ref):
    o_ref[...] = _dot(a_ref[...], w_ref[...]).astype(o_ref.dtype)


def _matmul(a, w, out_dtype):
    M, K = a.shape
    N = w.shape[1]
    bm = _tile(M, 1024, SUBLANE)
    bn = _tile(N, 1024, LANE)
    return pl.pallas_call(
        _mm_kernel,
        out_shape=jax.ShapeDtypeStruct((M, N), out_dtype),
        grid=(M // bm, N // bn),
        in_specs=[pl.BlockSpec((bm, K), lambda i, j: (i, 0)),
                  pl.BlockSpec((K, bn), lambda i, j: (0, j))],
        out_specs=pl.BlockSpec((bm, bn), lambda i, j: (i, j)),
        compiler_params=_params(("parallel", "parallel")),
        name="matmul",
    )(a, w)


def _mm_res_kernel(a_ref, w_ref, x_ref, g_ref, o_ref):
    o_ref[...] = x_ref[...] + g_ref[...] * _dot(a_ref[...], w_ref[...])


def _matmul_residual(a, w, x2, mod3, k_gate, seq, *, blk):
    M, K = a.shape
    N = w.shape[1]
    bm = _tile(seq, blk, SUBLANE)
    bn = _tile(N, blk, LANE)
    nb = N // bn
    per_seq = seq // bm
    return pl.pallas_call(
        _mm_res_kernel,
        out_shape=jax.ShapeDtypeStruct((M, N), F32),
        grid=(M // bm, nb),
        in_specs=[
            pl.BlockSpec((bm, K), lambda i, j: (i, 0)),
            pl.BlockSpec((K, bn), lambda i, j: (0, j)),
            pl.BlockSpec((bm, bn), lambda i, j: (i, j)),
            pl.BlockSpec((None, 1, bn), lambda i, j: (i // per_seq, 0, k_gate * nb + j)),
        ],
        out_specs=pl.BlockSpec((bm, bn), lambda i, j: (i, j)),
        compiler_params=_params(("parallel", "parallel")),
        name="matmul_residual",
    )(a, w, x2, mod3)


def _causal_conv3(p, halo, w_ref):
    row = lax.broadcasted_iota(jnp.int32, p.shape, 0)
    s1 = pltpu.roll(p, 1, axis=0)
    s2 = pltpu.roll(p, 2, axis=0)
    h7 = halo[SUBLANE - 1:SUBLANE, :]
    h6 = halo[SUBLANE - 2:SUBLANE - 1, :]
    s1 = jnp.where(row == 0, h7, s1)
    s2 = jnp.where(row == 0, h6, jnp.where(row == 1, h7, s2))
    return w_ref[0:1, :] * s2 + w_ref[1:2, :] * s1 + w_ref[2:3, :] * p


def _load_halo(halo_ref, j, first):
    h = halo_ref[j]
    return jnp.where(first, jnp.zeros_like(h), h)


def _sconv_in_kernel(a_ref, wb_ref, wc_ref, wx_ref, cw_ref, o_ref, halo_ref, *, per_seq):
    i = pl.program_id(0)
    j = pl.program_id(1)
    a = a_ref[...]
    p = _dot(a, wc_ref[...]) * _dot(a, wx_ref[...])
    halo = _load_halo(halo_ref, j, i % per_seq == 0)
    halo_ref[j] = p[p.shape[0] - SUBLANE:, :]
    o_ref[...] = (_dot(a, wb_ref[...]) * _causal_conv3(p, halo, cw_ref)).astype(o_ref.dtype)


def _sconv_in(a, w_in, conv_w3, seq):
    M, D = a.shape
    bm = _tile(seq, 1024, SUBLANE)
    bn = _tile(D, 256, LANE)
    nb = D // bn
    return pl.pallas_call(
        functools.partial(_sconv_in_kernel, per_seq=seq // bm),
        out_shape=jax.ShapeDtypeStruct((M, D), BF16),
        grid=(M // bm, nb),
        in_specs=[
            pl.BlockSpec((bm, D), lambda i, j: (i, 0)),
            pl.BlockSpec((D, bn), lambda i, j: (0, j)),
            pl.BlockSpec((D, bn), lambda i, j: (0, nb + j)),
            pl.BlockSpec((D, bn), lambda i, j: (0, 2 * nb + j)),
            pl.BlockSpec((3, bn), lambda i, j: (0, j)),
        ],
        out_specs=pl.BlockSpec((bm, bn), lambda i, j: (i, j)),
        scratch_shapes=[pltpu.VMEM((nb, SUBLANE, bn), F32)],
        compiler_params=_params(("arbitrary", "arbitrary")),
        name="sconv_in",
    )(a, w_in, w_in, w_in, conv_w3)


def _ffn_up_kernel(a_ref, wa_ref, wu_ref, cw_ref, cb_ref, o_ref, halo_ref, *, per_seq):
    i = pl.program_id(0)
    j = pl.program_id(1)
    a = a_ref[...]
    p = _dot(a, wa_ref[...])
    halo = _load_halo(halo_ref, j, i % per_seq == 0)
    halo_ref[j] = p[p.shape[0] - SUBLANE:, :]
    gate = _silu(_causal_conv3(p, halo, cw_ref) + cb_ref[...])
    o_ref[...] = (gate * _dot(a, wu_ref[...])).astype(o_ref.dtype)


def _ffn_up(a, w_up, conv_w3, conv_b, seq):
    M, D = a.shape
    F = w_up.shape[1] // 2
    bm = _tile(seq, 1024, SUBLANE)
    bn = _tile(F, FFN_BN, 2 * LANE)
    nb = F // bn
    return pl.pallas_call(
        functools.partial(_ffn_up_kernel, per_seq=seq // bm),
        out_shape=jax.ShapeDtypeStruct((M, F), BF16),
        grid=(M // bm, nb),
        in_specs=[
            pl.BlockSpec((bm, D), lambda i, j: (i, 0)),
            pl.BlockSpec((D, bn), lambda i, j: (0, j)),
            pl.BlockSpec((D, bn), lambda i, j: (0, nb + j)),
            pl.BlockSpec((3, bn), lambda i, j: (0, j)),
            pl.BlockSpec((1, bn), lambda i, j: (0, j)),
        ],
        out_specs=pl.BlockSpec((bm, bn), lambda i, j: (i, j)),
        scratch_shapes=[pltpu.VMEM((nb, SUBLANE, bn), F32)],
        compiler_params=_params(("arbitrary", "arbitrary")),
        name="ffn_up",
    )(a, w_up, w_up, conv_w3, conv_b)


N_FINE = 3
LOG_SPLIT = 2


def _hgrn_constants(C, W):
    u = np.arange(C)[None, :]
    t = np.arange(C)[:, None]
    bands = [u <= t]
    masks = []
    for l in range(N_FINE):
        h = 1 << l
        m = (t // (2 * h)) * (2 * h) + h
        upper = t >= m
        bands.append(np.where(upper, (u > m) & (u <= t), (u > t) & (u <= m)))
        masks.append(np.broadcast_to(upper[:SUBLANE], (SUBLANE, W)))
    d = np.concatenate(bands, axis=0).astype(np.float32)
    dmat = np.concatenate([d] * LOG_SPLIT, axis=1)
    rowmask = np.concatenate(masks, axis=0).astype(np.float32)
    s = np.arange(2 * C)[None, :] % C
    x = t ^ s
    lev = np.where(t == s, 0, np.where(t > s, np.floor(np.log2(np.maximum(x, 1))) + 1, -1))
    hh = np.arange(W) // HEAD
    mean = (hh[:, None] == hh[None, :]).astype(np.float32) / HEAD
    return (jnp.asarray(dmat, BF16), jnp.asarray(rowmask, F32),
            jnp.asarray(lev, jnp.int32), jnp.asarray(mean, BF16))


def _hgrn_kernel(q_ref, f_ref, v_ref, g_ref, lbl_ref, gn_ref, dmat_ref, rmask_ref, lev_ref,
                 mean_ref, o_ref, st_ref, *, n_chunks, slot, group):
    C = CHUNK
    W = 2 * HEAD
    n_lvl = int(np.log2(C))

    @pl.when(pl.program_id(2) == 0)
    def _():
        st_ref[...] = jnp.zeros_like(st_ref)

    rows = [lbl_ref[k:k + 1, :] for k in range(lbl_ref.shape[0])]
    mx = functools.reduce(jnp.maximum, rows)
    ex = [jnp.exp(r - mx) for r in rows]
    lb = functools.reduce(jnp.add, ex[:slot + 1]) / functools.reduce(jnp.add, ex)
    gn = gn_ref[...]
    zc = jnp.zeros((C, HEAD), BF16)
    zs = jnp.zeros((HEAD, HEAD), BF16)

    def by_head(x):
        return jnp.concatenate(
            [jnp.concatenate([x[:, :HEAD], zc], axis=1),
             jnp.concatenate([zc, x[:, HEAD:]], axis=1)], axis=0)

    G = group
    R = G * C

    def per_chunk(x):
        return [x[c * C:(c + 1) * C] for c in range(G)]

    def body(gi, carry):
        rs = pl.ds(pl.multiple_of(gi * R, R), R)
        q_all = q_ref[rs, :]
        f_all = lb + (1.0 - lb) * jax.nn.sigmoid(f_ref[rs, :])
        k_all = 1.0 - f_all
        lg = jnp.log(f_all) * LOG2E
        pieces, rem = [], lg
        for _ in range(LOG_SPLIT - 1):
            pieces.append(rem.astype(BF16))
            rem = rem - pieces[-1].astype(F32)
        pieces.append(rem.astype(BF16))
        q, k = per_chunk(q_all), per_chunk(k_all)
        vb = per_chunk(v_ref[rs, :].astype(BF16))
        dmat = dmat_ref[...]
        args = [_dot(dmat, jnp.concatenate(list(xs), axis=0))
                for xs in zip(*[per_chunk(x) for x in pieces])]
        b = [x[0:C] for x in args]
        b_last = [x[C - 1:C, :] for x in b]
        e_fine = [jnp.exp2(x[C:]) for x in args]
        lev = lev_ref[...]

        a = [jnp.where(lev == 0, _dot_nt(q[c].astype(BF16), by_head(k[c].astype(BF16))), 0.0)
             for c in range(G)]
        for l in range(N_FINE):
            up = jnp.tile(rmask_ref[l * SUBLANE:(l + 1) * SUBLANE, :], (C // SUBLANE, 1)) != 0.0
            for c in range(G):
                z = (jnp.where(up, q[c], k[c]) * e_fine[c][l * C:(l + 1) * C]).astype(BF16)
                a[c] = jnp.where(lev == l + 1, _dot_nt(z, by_head(z)), a[c])
        for l in range(N_FINE, n_lvl):
            h = 1 << l
            zh = jnp.zeros((h, W), BF16)
            for c in range(G):
                qe, ke = [], []
                for base in range(0, C, 2 * h):
                    m = base + h
                    bm = b[c][m:m + 1, :]
                    qe += [zh, (q[c][m:m + h] * jnp.exp2(b[c][m:m + h] - bm)).astype(BF16)]
                    ke += [(k[c][base:m] * jnp.exp2(bm - b[c][base:m])).astype(BF16), zh]
                p = _dot_nt(jnp.concatenate(qe, axis=0), by_head(jnp.concatenate(ke, axis=0)))
                a[c] = jnp.where(lev == l + 1, p, a[c])
        o_intra = [_dot(a[c].astype(BF16), by_head(vb[c])) for c in range(G)]

        qh = [(q[c] * jnp.exp2(b[c])).astype(BF16) for c in range(G)]
        upd = [_dot_tn(vb[c], (k[c] * jnp.exp2(b_last[c] - b[c])).astype(BF16))
               for c in range(G)]
        d_s = [jnp.exp2(x) for x in b_last]
        s0 = st_ref[0]
        s1 = st_ref[1]
        o = []
        for c in range(G):
            s_bd = jnp.concatenate(
                [jnp.concatenate([s0.astype(BF16), zs], axis=1),
                 jnp.concatenate([zs, s1.astype(BF16)], axis=1)], axis=0)
            o.append(_dot_nt(qh[c], s_bd) + o_intra[c])
            s0 = s0 * d_s[c][:, :HEAD] + upd[c][:HEAD, :HEAD]
            s1 = s1 * d_s[c][:, HEAD:] + upd[c][HEAD:, HEAD:]
        st_ref[0] = s0
        st_ref[1] = s1

        o_all = jnp.concatenate(o, axis=0)
        ms = _dot((o_all * o_all).astype(BF16), mean_ref[...])
        g = g_ref[rs, :]
        o_ref[rs, :] = (o_all * lax.rsqrt(ms + EPS) * gn * _silu(g)).astype(o_ref.dtype)
        return carry

    lax.fori_loop(0, n_chunks // G, body, 0)


def _hgrn_recurrence(proj, lb_logits, g_norm, batch, seq, slot):
    M, D4 = proj.shape
    D = D4 // 4
    W = 2 * HEAD
    npair = D // W
    tb = _tile(seq, 1024, CHUNK)
    nt = seq // tb
    group = _tile(tb // CHUNK, 8, 1)
    consts = _hgrn_constants(CHUNK, W)

    def col(part):
        return pl.BlockSpec((tb, W), lambda b, p, t: (b * nt + t, part * npair + p))

    def whole(arr):
        return pl.BlockSpec(arr.shape, lambda b, p, t: (0, 0))

    return pl.pallas_call(
        functools.partial(_hgrn_kernel, n_chunks=tb // CHUNK, slot=slot, group=group),
        out_shape=jax.ShapeDtypeStruct((M, D), BF16),
        grid=(batch, npair, nt),
        in_specs=[col(0), col(1), col(2), col(3),
                  pl.BlockSpec((lb_logits.shape[0], W), lambda b, p, t: (0, p)),
                  pl.BlockSpec((1, W), lambda b, p, t: (0, p))] + [whole(c) for c in consts],
        out_specs=pl.BlockSpec((tb, W), lambda b, p, t: (b * nt + t, p)),
        scratch_shapes=[pltpu.VMEM((2, HEAD, HEAD), F32)],
        compiler_params=_params(("parallel", "parallel", "arbitrary")),
        name="hgrn_recurrence",
    )(proj, proj, proj, proj, lb_logits, g_norm.reshape(1, D), *consts)


def kernel(x, c, w_cond, b_cond, ada_w, ada_b, lb_logits, hgrn_w_in, hgrn_norm, hgrn_w_out,
           sconv_w_in, sconv_conv_w, sconv_w_out, ffn_w_up, ffn_conv_w, ffn_conv_b,
           ffn_w_down, final_norm):
    B, T, D = x.shape
    depth = ada_w.shape[0]
    F = ffn_conv_b.shape[1]
    M = B * T
    n_mixers = 2
    Fp = -(-F // FFN_BN) * FFN_BN

    c_emb = _cond_matmul(c, w_cond[None], b_cond[None], act=True)[0]
    mod = _cond_matmul(c_emb, ada_w, ada_b, act=False)

    x2 = x.reshape(M, D)
    for i in range(depth):
        mod3 = mod[i].reshape(B, 1, N_MOD * D)
        j = i // n_mixers
        hm = _prenorm(x2.reshape(B, T, D), mod3, 0, 1).reshape(M, D)
        if i % n_mixers == 0:
            proj = _matmul(hm, hgrn_w_in[j].astype(BF16), F32)
            y = _hgrn_recurrence(proj, lb_logits, hgrn_norm[j], B, T, slot=i)
            w_out = hgrn_w_out[j]
        else:
            y = _sconv_in(hm, sconv_w_in[j].astype(BF16), sconv_conv_w[j].T, T)
            w_out = sconv_w_out[j]
        x2 = _matmul_residual(y, w_out.astype(BF16), x2, mod3, 2, T, blk=1024)

        hf = _prenorm(x2.reshape(B, T, D), mod3, 3, 4).reshape(M, D)
        w_up = jnp.pad(ffn_w_up[i].reshape(D, 2, F).astype(BF16),
                       ((0, 0), (0, 0), (0, Fp - F))).reshape(D, 2 * Fp)
        cw = jnp.pad(ffn_conv_w[i].T, ((0, 0), (0, Fp - F)))
        cb = jnp.pad(ffn_conv_b[i].reshape(1, F), ((0, 0), (0, Fp - F)))
        w_down = jnp.pad(ffn_w_down[i].astype(BF16), ((0, Fp - F), (0, 0)))
        h = _ffn_up(hf, w_up, cw, cb, T)
        x2 = _matmul_residual(h, w_down, x2, mod3, 5, T, blk=512)
    return _final_norm(x2, final_norm).reshape(B, T, D)
```

```python
import functools

import numpy as np
import jax
import jax.numpy as jnp
from jax import lax
from jax.experimental import pallas as pl
from jax.experimental.pallas import tpu as pltpu

EPS = 1e-6
CHUNK = 64
HEAD = 128
N_MOD = 6
LOG2E = 1.4426950408889634
F32 = jnp.float32
BF16 = jnp.bfloat16

V7X_VMEM_BYTES = 64 * 1024 * 1024
VMEM_LIMIT = V7X_VMEM_BYTES - 8 * 1024 * 1024
LANE = 128
SUBLANE = 8
FFN_BN = 512


def _tile(n, pref, align):
    t = min(pref, n)
    t -= t % align
    while t >= align:
        if n % t == 0:
            return t
        t -= align
    return n


def _params(sem):
    return pltpu.CompilerParams(dimension_semantics=sem, vmem_limit_bytes=VMEM_LIMIT)


def _dot(a, b):
    return jnp.dot(a, b, preferred_element_type=F32)


def _dot_nt(a, b):
    return lax.dot_general(a, b, (((1,), (1,)), ((), ())), preferred_element_type=F32)


def _dot_tn(a, b):
    return lax.dot_general(a, b, (((0,), (0,)), ((), ())), preferred_element_type=F32)


def _silu(x):
    return x * jax.nn.sigmoid(x)


def _cond_kernel(c_ref, w_ref, b_ref, o_ref, *, act):
    acc = _dot(c_ref[...].astype(BF16), w_ref[...].astype(BF16)) + b_ref[...]
    o_ref[...] = _silu(acc) if act else acc


def _cond_matmul(c, w, b, *, act):
    L, K, N = w.shape
    B = c.shape[0]
    bn = _tile(N, 512, LANE)
    return pl.pallas_call(
        functools.partial(_cond_kernel, act=act),
        out_shape=jax.ShapeDtypeStruct((L, B, N), F32),
        grid=(L, N // bn),
        in_specs=[
            pl.BlockSpec((B, K), lambda l, j: (0, 0)),
            pl.BlockSpec((None, K, bn), lambda l, j: (l, 0, j)),
            pl.BlockSpec((None, 1, bn), lambda l, j: (l, 0, j)),
        ],
        out_specs=pl.BlockSpec((None, B, bn), lambda l, j: (l, 0, j)),
        compiler_params=_params(("parallel", "parallel")),
        name="cond_matmul",
    )(c, w, b.reshape(L, 1, N))


def _prenorm_kernel(x_ref, sh_ref, sc_ref, o_ref):
    x = x_ref[...]
    y = x * lax.rsqrt(jnp.mean(x * x, axis=-1, keepdims=True) + EPS)
    o_ref[...] = (y * (1.0 + sc_ref[...]) + sh_ref[...]).astype(o_ref.dtype)


def _prenorm(x3, mod3, k_shift, k_scale):
    B, T, D = x3.shape
    bt = _tile(T, 256, SUBLANE)
    return pl.pallas_call(
        _prenorm_kernel,
        out_shape=jax.ShapeDtypeStruct((B, T, D), BF16),
        grid=(B, T // bt),
        in_specs=[
            pl.BlockSpec((None, bt, D), lambda b, t: (b, t, 0)),
            pl.BlockSpec((None, 1, D), lambda b, t: (b, 0, k_shift)),
            pl.BlockSpec((None, 1, D), lambda b, t: (b, 0, k_scale)),
        ],
        out_specs=pl.BlockSpec((None, bt, D), lambda b, t: (b, t, 0)),
        compiler_params=_params(("parallel", "parallel")),
        name="prenorm",
    )(x3, mod3, mod3)


def _final_norm_kernel(x_ref, w_ref, o_ref):
    x = x_ref[...]
    o_ref[...] = x * lax.rsqrt(jnp.mean(x * x, axis=-1, keepdims=True) + EPS) * w_ref[...]


def _final_norm(x2, w):
    M, D = x2.shape
    bt = _tile(M, 256, SUBLANE)
    return pl.pallas_call(
        _final_norm_kernel,
        out_shape=jax.ShapeDtypeStruct((M, D), F32),
        grid=(M // bt,),
        in_specs=[pl.BlockSpec((bt, D), lambda i: (i, 0)),
                  pl.BlockSpec((1, D), lambda i: (0, 0))],
        out_specs=pl.BlockSpec((bt, D), lambda i: (i, 0)),
        compiler_params=_params(("parallel",)),
        name="final_norm",
    )(x2, w.reshape(1, D))


def _mm_kernel(a_ref, w_ref, o_ref):
    o_ref[...] = _dot(a_ref[...], w_ref[...]).astype(o_ref.dtype)


def _matmul(a, w, layer, out_dtype):
    M, K = a.shape
    N = w.shape[2]
    bm = _tile(M, 1024, SUBLANE)
    bn = _tile(N, 1024, LANE)
    return pl.pallas_call(
        _mm_kernel,
        out_shape=jax.ShapeDtypeStruct((M, N), out_dtype),
        grid=(M // bm, N // bn),
        in_specs=[pl.BlockSpec((bm, K), lambda i, j: (i, 0)),
                  pl.BlockSpec((None, K, bn), lambda i, j: (layer, 0, j))],
        out_specs=pl.BlockSpec((bm, bn), lambda i, j: (i, j)),
        compiler_params=_params(("parallel", "parallel")),
        name="matmul",
    )(a, w)


def _mm_res_kernel(a_ref, w_ref, x_ref, g_ref, o_ref):
    o_ref[...] = x_ref[...] + g_ref[...] * _dot(a_ref[...], w_ref[...])


def _matmul_residual(a, w, layer, x2, mod3, k_gate, seq, *, blk):
    M, K = a.shape
    N = w.shape[2]
    bm = _tile(seq, blk, SUBLANE)
    bn = _tile(N, blk, LANE)
    nb = N // bn
    per_seq = seq // bm
    return pl.pallas_call(
        _mm_res_kernel,
        out_shape=jax.ShapeDtypeStruct((M, N), F32),
        grid=(M // bm, nb),
        in_specs=[
            pl.BlockSpec((bm, K), lambda i, j: (i, 0)),
            pl.BlockSpec((None, K, bn), lambda i, j: (layer, 0, j)),
            pl.BlockSpec((bm, bn), lambda i, j: (i, j)),
            pl.BlockSpec((None, 1, bn), lambda i, j: (i // per_seq, 0, k_gate * nb + j)),
        ],
        out_specs=pl.BlockSpec((bm, bn), lambda i, j: (i, j)),
        compiler_params=_params(("parallel", "parallel")),
        name="matmul_residual",
    )(a, w, x2, mod3)


def _causal_conv3(p, halo, w_ref):
    row = lax.broadcasted_iota(jnp.int32, p.shape, 0)
    s1 = pltpu.roll(p, 1, axis=0)
    s2 = pltpu.roll(p, 2, axis=0)
    h7 = halo[SUBLANE - 1:SUBLANE, :]
    h6 = halo[SUBLANE - 2:SUBLANE - 1, :]
    s1 = jnp.where(row == 0, h7, s1)
    s2 = jnp.where(row == 0, h6, jnp.where(row == 1, h7, s2))
    return w_ref[0:1, :] * s2 + w_ref[1:2, :] * s1 + w_ref[2:3, :] * p


def _load_halo(halo_ref, j, first):
    h = halo_ref[j]
    return jnp.where(first, jnp.zeros_like(h), h)


def _sconv_in_kernel(a_ref, wb_ref, wc_ref, wx_ref, cw_ref, o_ref, halo_ref, *, per_seq):
    i = pl.program_id(0)
    j = pl.program_id(1)
    a = a_ref[...]
    p = _dot(a, wc_ref[...]) * _dot(a, wx_ref[...])
    halo = _load_halo(halo_ref, j, i % per_seq == 0)
    halo_ref[j] = p[p.shape[0] - SUBLANE:, :]
    o_ref[...] = (_dot(a, wb_ref[...]) * _causal_conv3(p, halo, cw_ref)).astype(o_ref.dtype)


def _sconv_in(a, w_in, layer, conv_w3, seq):
    M, D = a.shape
    bm = _tile(seq, 1024, SUBLANE)
    bn = _tile(D, 256, LANE)
    nb = D // bn
    return pl.pallas_call(
        functools.partial(_sconv_in_kernel, per_seq=seq // bm),
        out_shape=jax.ShapeDtypeStruct((M, D), BF16),
        grid=(M // bm, nb),
        in_specs=[
            pl.BlockSpec((bm, D), lambda i, j: (i, 0)),
            pl.BlockSpec((None, D, bn), lambda i, j: (layer, 0, j)),
            pl.BlockSpec((None, D, bn), lambda i, j: (layer, 0, nb + j)),
            pl.BlockSpec((None, D, bn), lambda i, j: (layer, 0, 2 * nb + j)),
            pl.BlockSpec((3, bn), lambda i, j: (0, j)),
        ],
        out_specs=pl.BlockSpec((bm, bn), lambda i, j: (i, j)),
        scratch_shapes=[pltpu.VMEM((nb, SUBLANE, bn), F32)],
        compiler_params=_params(("arbitrary", "arbitrary")),
        name="sconv_in",
    )(a, w_in, w_in, w_in, conv_w3)


def _ffn_up_kernel(a_ref, wa_ref, wu0_ref, wu1_ref, cw_ref, cb_ref, o_ref, halo_ref, *, per_seq):
    i = pl.program_id(0)
    j = pl.program_id(1)
    a = a_ref[...]
    p = _dot(a, wa_ref[...])
    halo = _load_halo(halo_ref, j, i % per_seq == 0)
    halo_ref[j] = p[p.shape[0] - SUBLANE:, :]
    gate = _silu(_causal_conv3(p, halo, cw_ref) + cb_ref[...])
    u = jnp.concatenate([_dot(a, wu0_ref[...]), _dot(a, wu1_ref[...])], axis=1)
    o_ref[...] = (gate * u).astype(o_ref.dtype)


def _ffn_up(a, w_up, layer, conv_w3, conv_b, seq):
    M, D = a.shape
    F = w_up.shape[2] // 2
    bm = _tile(seq, 1024, SUBLANE)
    bn = FFN_BN
    half = bn // 2
    assert F % half == 0
    nb = pl.cdiv(F, bn)
    u0 = F // half
    last = 2 * F // half - 1
    return pl.pallas_call(
        functools.partial(_ffn_up_kernel, per_seq=seq // bm),
        out_shape=jax.ShapeDtypeStruct((M, F), BF16),
        grid=(M // bm, nb),
        in_specs=[
            pl.BlockSpec((bm, D), lambda i, j: (i, 0)),
            pl.BlockSpec((None, D, bn), lambda i, j: (layer, 0, j)),
            pl.BlockSpec((None, D, half), lambda i, j: (layer, 0, u0 + 2 * j)),
            pl.BlockSpec((None, D, half), lambda i, j: (layer, 0, jnp.minimum(u0 + 2 * j + 1, last))),
            pl.BlockSpec((3, bn), lambda i, j: (0, j)),
            pl.BlockSpec((1, bn), lambda i, j: (0, j)),
        ],
        out_specs=pl.BlockSpec((bm, bn), lambda i, j: (i, j)),
        scratch_shapes=[pltpu.VMEM((nb, SUBLANE, bn), F32)],
        compiler_params=_params(("arbitrary", "arbitrary")),
        name="ffn_up",
    )(a, w_up, w_up, w_up, conv_w3, conv_b)


N_FINE = 3
LOG_SPLIT = 2


def _hgrn_constants(C, W):
    u = np.arange(C)[None, :]
    t = np.arange(C)[:, None]
    bands = [u <= t]
    masks = []
    for l in range(N_FINE):
        h = 1 << l
        m = (t // (2 * h)) * (2 * h) + h
        upper = t >= m
        bands.append(np.where(upper, (u > m) & (u <= t), (u > t) & (u <= m)))
        masks.append(np.broadcast_to(upper[:SUBLANE], (SUBLANE, W)))
    d = np.concatenate(bands, axis=0).astype(np.float32)
    dmat = np.concatenate([d] * LOG_SPLIT, axis=1)
    rowmask = np.concatenate(masks, axis=0).astype(np.float32)
    s = np.arange(2 * C)[None, :] % C
    x = t ^ s
    lev = np.where(t == s, 0, np.where(t > s, np.floor(np.log2(np.maximum(x, 1))) + 1, -1))
    hh = np.arange(W) // HEAD
    mean = (hh[:, None] == hh[None, :]).astype(np.float32) / HEAD
    return (jnp.asarray(dmat, BF16), jnp.asarray(rowmask, F32),
            jnp.asarray(lev, jnp.int32), jnp.asarray(mean, BF16))


def _hgrn_kernel(q_ref, f_ref, v_ref, g_ref, lbl_ref, gn_ref, dmat_ref, rmask_ref, lev_ref,
                 mean_ref, o_ref, st_ref, *, n_chunks, slot, group):
    C = CHUNK
    W = 2 * HEAD
    n_lvl = int(np.log2(C))

    @pl.when(pl.program_id(2) == 0)
    def _():
        st_ref[...] = jnp.zeros_like(st_ref)

    rows = [lbl_ref[k:k + 1, :] for k in range(lbl_ref.shape[0])]
    mx = functools.reduce(jnp.maximum, rows)
    ex = [jnp.exp(r - mx) for r in rows]
    lb = functools.reduce(jnp.add, ex[:slot + 1]) / functools.reduce(jnp.add, ex)
    gn = gn_ref[...]
    zc = jnp.zeros((C, HEAD), BF16)
    zs = jnp.zeros((HEAD, HEAD), BF16)

    def by_head(x):
        return jnp.concatenate(
            [jnp.concatenate([x[:, :HEAD], zc], axis=1),
             jnp.concatenate([zc, x[:, HEAD:]], axis=1)], axis=0)

    G = group
    R = G * C

    def per_chunk(x):
        return [x[c * C:(c + 1) * C] for c in range(G)]

    def body(gi, carry):
        rs = pl.ds(pl.multiple_of(gi * R, R), R)
        q_all = q_ref[rs, :]
        f_all = lb + (1.0 - lb) * jax.nn.sigmoid(f_ref[rs, :])
        k_all = 1.0 - f_all
        lg = jnp.log(f_all) * LOG2E
        pieces, rem = [], lg
        for _ in range(LOG_SPLIT - 1):
            pieces.append(rem.astype(BF16))
            rem = rem - pieces[-1].astype(F32)
        pieces.append(rem.astype(BF16))
        q, k = per_chunk(q_all), per_chunk(k_all)
        vb = per_chunk(v_ref[rs, :].astype(BF16))
        dmat = dmat_ref[...]
        args = [_dot(dmat, jnp.concatenate(list(xs), axis=0))
                for xs in zip(*[per_chunk(x) for x in pieces])]
        b = [x[0:C] for x in args]
        b_last = [x[C - 1:C, :] for x in b]
        e_fine = [jnp.exp2(x[C:]) for x in args]
        lev = lev_ref[...]

        a = [jnp.where(lev == 0, _dot_nt(q[c].astype(BF16), by_head(k[c].astype(BF16))), 0.0)
             for c in range(G)]
        for l in range(N_FINE):
            up = jnp.tile(rmask_ref[l * SUBLANE:(l + 1) * SUBLANE, :], (C // SUBLANE, 1)) != 0.0
            for c in range(G):
                z = (jnp.where(up, q[c], k[c]) * e_fine[c][l * C:(l + 1) * C]).astype(BF16)
                a[c] = jnp.where(lev == l + 1, _dot_nt(z, by_head(z)), a[c])
        for l in range(N_FINE, n_lvl):
            h = 1 << l
            zh = jnp.zeros((h, W), BF16)
            for c in range(G):
                qe, ke = [], []
                for base in range(0, C, 2 * h):
                    m = base + h
                    bm = b[c][m:m + 1, :]
                    qe += [zh, (q[c][m:m + h] * jnp.exp2(b[c][m:m + h] - bm)).astype(BF16)]
                    ke += [(k[c][base:m] * jnp.exp2(bm - b[c][base:m])).astype(BF16), zh]
                p = _dot_nt(jnp.concatenate(qe, axis=0), by_head(jnp.concatenate(ke, axis=0)))
                a[c] = jnp.where(lev == l + 1, p, a[c])
        o_intra = [_dot(a[c].astype(BF16), by_head(vb[c])) for c in range(G)]

        qh = [(q[c] * jnp.exp2(b[c])).astype(BF16) for c in range(G)]
        kh = [(k[c] * jnp.exp2(b_last[c] - b[c])).astype(BF16) for c in range(G)]
        upd0 = [_dot_tn(vb[c][:, :HEAD], kh[c][:, :HEAD]) for c in range(G)]
        upd1 = [_dot_tn(vb[c][:, HEAD:], kh[c][:, HEAD:]) for c in range(G)]
        d_s = [jnp.exp2(x) for x in b_last]
        s0 = st_ref[0]
        s1 = st_ref[1]
        o = []
        for c in range(G):
            s_bd = jnp.concatenate(
                [jnp.concatenate([s0.astype(BF16), zs], axis=1),
                 jnp.concatenate([zs, s1.astype(BF16)], axis=1)], axis=0)
            o.append(_dot_nt(qh[c], s_bd) + o_intra[c])
            s0 = s0 * d_s[c][:, :HEAD] + upd0[c]
            s1 = s1 * d_s[c][:, HEAD:] + upd1[c]
        st_ref[0] = s0
        st_ref[1] = s1

        o_all = jnp.concatenate(o, axis=0)
        ms = _dot((o_all * o_all).astype(BF16), mean_ref[...])
        g = g_ref[rs, :]
        o_ref[rs, :] = (o_all * lax.rsqrt(ms + EPS) * gn * _silu(g)).astype(o_ref.dtype)
        return carry

    lax.fori_loop(0, n_chunks // G, body, 0)


def _hgrn_recurrence(proj, lb_logits, g_norm, batch, seq, slot):
    M, D4 = proj.shape
    D = D4 // 4
    W = 2 * HEAD
    npair = D // W
    tb = _tile(seq, 1024, CHUNK)
    nt = seq // tb
    group = _tile(tb // CHUNK, 8, 1)
    consts = _hgrn_constants(CHUNK, W)

    def col(part):
        return pl.BlockSpec((tb, W), lambda b, p, t: (b * nt + t, part * npair + p))

    def whole(arr):
        return pl.BlockSpec(arr.shape, lambda b, p, t: (0, 0))

    return pl.pallas_call(
        functools.partial(_hgrn_kernel, n_chunks=tb // CHUNK, slot=slot, group=group),
        out_shape=jax.ShapeDtypeStruct((M, D), BF16),
        grid=(batch, npair, nt),
        in_specs=[col(0), col(1), col(2), col(3),
                  pl.BlockSpec((lb_logits.shape[0], W), lambda b, p, t: (0, p)),
                  pl.BlockSpec((1, W), lambda b, p, t: (0, p))] + [whole(c) for c in consts],
        out_specs=pl.BlockSpec((tb, W), lambda b, p, t: (b * nt + t, p)),
        scratch_shapes=[pltpu.VMEM((2, HEAD, HEAD), F32)],
        compiler_params=_params(("parallel", "parallel", "arbitrary")),
        name="hgrn_recurrence",
    )(proj, proj, proj, proj, lb_logits, g_norm.reshape(1, D), *consts)


def kernel(x, c, w_cond, b_cond, ada_w, ada_b, lb_logits, hgrn_w_in, hgrn_norm, hgrn_w_out,
           sconv_w_in, sconv_conv_w, sconv_w_out, ffn_w_up, ffn_conv_w, ffn_conv_b,
           ffn_w_down, final_norm):
    B, T, D = x.shape
    depth = ada_w.shape[0]
    F = ffn_conv_b.shape[1]
    M = B * T
    n_mixers = 2

    c_emb = _cond_matmul(c, w_cond[None], b_cond[None], act=True)[0]
    mod = _cond_matmul(c_emb, ada_w, ada_b, act=False)

    w_hin, w_hout = hgrn_w_in.astype(BF16), hgrn_w_out.astype(BF16)
    w_sin, w_sout = sconv_w_in.astype(BF16), sconv_w_out.astype(BF16)
    w_up, w_down = ffn_w_up.astype(BF16), ffn_w_down.astype(BF16)

    x2 = x.reshape(M, D)
    for i in range(depth):
        mod3 = mod[i].reshape(B, 1, N_MOD * D)
        j = i // n_mixers
        hm = _prenorm(x2.reshape(B, T, D), mod3, 0, 1).reshape(M, D)
        if i % n_mixers == 0:
            proj = _matmul(hm, w_hin, j, F32)
            y = _hgrn_recurrence(proj, lb_logits, hgrn_norm[j], B, T, slot=i)
            x2 = _matmul_residual(y, w_hout, j, x2, mod3, 2, T, blk=1024)
        else:
            y = _sconv_in(hm, w_sin, j, sconv_conv_w[j].T, T)
            x2 = _matmul_residual(y, w_sout, j, x2, mod3, 2, T, blk=1024)

        hf = _prenorm(x2.reshape(B, T, D), mod3, 3, 4).reshape(M, D)
        h = _ffn_up(hf, w_up, i, ffn_conv_w[i].T, ffn_conv_b[i].reshape(1, F), T)
        x2 = _matmul_residual(h, w_down, i, x2, mod3, 5, T, blk=512)
    return _final_norm(x2, final_norm).reshape(B, T, D)
```

```python
import functools

import numpy as np
import jax
import jax.numpy as jnp
from jax import lax
from jax.experimental import pallas as pl
from jax.experimental.pallas import tpu as pltpu

EPS = 1e-6
CHUNK = 64
HEAD = 128
N_MOD = 6
LOG2E = 1.4426950408889634
F32 = jnp.float32
BF16 = jnp.bfloat16

V7X_VMEM_BYTES = 64 * 1024 * 1024
VMEM_LIMIT = V7X_VMEM_BYTES - 8 * 1024 * 1024
LANE = 128
SUBLANE = 8
FFN_BN = 512
OUT_BM = 256


def _tile(n, pref, align):
    t = min(pref, n)
    t -= t % align
    while t >= align:
        if n % t == 0:
            return t
        t -= align
    return n


def _params(sem, vmem=VMEM_LIMIT):
    return pltpu.CompilerParams(dimension_semantics=sem, vmem_limit_bytes=vmem)


def _dot(a, b):
    return jnp.dot(a, b, preferred_element_type=F32)


def _dot_nt(a, b):
    return lax.dot_general(a, b, (((1,), (1,)), ((), ())), preferred_element_type=F32)


def _dot_tn(a, b):
    return lax.dot_general(a, b, (((0,), (0,)), ((), ())), preferred_element_type=F32)


def _silu(x):
    return x * jax.nn.sigmoid(x)


def _cond_kernel(c_ref, w_ref, b_ref, o_ref, *, act):
    acc = _dot(c_ref[...].astype(BF16), w_ref[...].astype(BF16)) + b_ref[...]
    o_ref[...] = _silu(acc) if act else acc


def _cond_matmul(c, w, b, *, act):
    L, K, N = w.shape
    B = c.shape[0]
    bn = _tile(N, 512, LANE)
    return pl.pallas_call(
        functools.partial(_cond_kernel, act=act),
        out_shape=jax.ShapeDtypeStruct((L, B, N), F32),
        grid=(L, N // bn),
        in_specs=[
            pl.BlockSpec((B, K), lambda l, j: (0, 0)),
            pl.BlockSpec((None, K, bn), lambda l, j: (l, 0, j)),
            pl.BlockSpec((None, 1, bn), lambda l, j: (l, 0, j)),
        ],
        out_specs=pl.BlockSpec((None, B, bn), lambda l, j: (l, 0, j)),
        compiler_params=_params(("parallel", "parallel")),
        name="cond_matmul",
    )(c, w, b.reshape(L, 1, N))


def _prenorm_kernel(x_ref, sh_ref, sc_ref, o_ref):
    x = x_ref[...]
    y = x * lax.rsqrt(jnp.mean(x * x, axis=-1, keepdims=True) + EPS)
    o_ref[...] = (y * (1.0 + sc_ref[...]) + sh_ref[...]).astype(o_ref.dtype)


def _prenorm(x3, mod3, k_shift, k_scale):
    B, T, D = x3.shape
    bt = _tile(T, 512, SUBLANE)
    return pl.pallas_call(
        _prenorm_kernel,
        out_shape=jax.ShapeDtypeStruct((B, T, D), BF16),
        grid=(B, T // bt),
        in_specs=[
            pl.BlockSpec((None, bt, D), lambda b, t: (b, t, 0)),
            pl.BlockSpec((None, 1, D), lambda b, t: (b, 0, k_shift)),
            pl.BlockSpec((None, 1, D), lambda b, t: (b, 0, k_scale)),
        ],
        out_specs=pl.BlockSpec((None, bt, D), lambda b, t: (b, t, 0)),
        compiler_params=_params(("parallel", "parallel")),
        name="prenorm",
    )(x3, mod3, mod3)


def _final_norm_kernel(x_ref, w_ref, o_ref):
    x = x_ref[...]
    o_ref[...] = x * lax.rsqrt(jnp.mean(x * x, axis=-1, keepdims=True) + EPS) * w_ref[...]


def _final_norm(x2, w):
    M, D = x2.shape
    bt = _tile(M, 512, SUBLANE)
    return pl.pallas_call(
        _final_norm_kernel,
        out_shape=jax.ShapeDtypeStruct((M, D), F32),
        grid=(M // bt,),
        in_specs=[pl.BlockSpec((bt, D), lambda i: (i, 0)),
                  pl.BlockSpec((1, D), lambda i: (0, 0))],
        out_specs=pl.BlockSpec((bt, D), lambda i: (i, 0)),
        compiler_params=_params(("parallel",)),
        name="final_norm",
    )(x2, w.reshape(1, D))


def _mm_kernel(a_ref, w_ref, o_ref):
    o_ref[...] = _dot(a_ref[...], w_ref[...]).astype(o_ref.dtype)


def _matmul(a, w, layer, out_dtype):
    M, K = a.shape
    N = w.shape[2]
    bm = _tile(M, 1024, SUBLANE)
    bn = _tile(N, 1024, LANE)
    return pl.pallas_call(
        _mm_kernel,
        out_shape=jax.ShapeDtypeStruct((M, N), out_dtype),
        grid=(M // bm, N // bn),
        in_specs=[pl.BlockSpec((bm, K), lambda i, j: (i, 0)),
                  pl.BlockSpec((None, K, bn), lambda i, j: (layer, 0, j))],
        out_specs=pl.BlockSpec((bm, bn), lambda i, j: (i, j)),
        compiler_params=_params(("parallel", "parallel")),
        name="matmul",
    )(a, w)


def _mm_res_kernel(a_ref, w_ref, x_ref, g_ref, o_ref):
    o_ref[...] = x_ref[...] + g_ref[...] * _dot(a_ref[...], w_ref[...])


def _matmul_residual(a, w, layer, x2, mod3, k_gate, seq, *, blk):
    M, K = a.shape
    N = w.shape[2]
    bm = _tile(seq, blk, SUBLANE)
    bn = _tile(N, blk, LANE)
    nb = N // bn
    per_seq = seq // bm
    return pl.pallas_call(
        _mm_res_kernel,
        out_shape=jax.ShapeDtypeStruct((M, N), F32),
        grid=(M // bm, nb),
        in_specs=[
            pl.BlockSpec((bm, K), lambda i, j: (i, 0)),
            pl.BlockSpec((None, K, bn), lambda i, j: (layer, 0, j)),
            pl.BlockSpec((bm, bn), lambda i, j: (i, j)),
            pl.BlockSpec((None, 1, bn), lambda i, j: (i // per_seq, 0, k_gate * nb + j)),
        ],
        out_specs=pl.BlockSpec((bm, bn), lambda i, j: (i, j)),
        compiler_params=_params(("parallel", "parallel")),
        name="matmul_residual",
    )(a, w, x2, mod3)


def _out_norm_kernel(a_ref, w_ref, x_ref, g_ref, sh_ref, sc_ref, xo_ref, ho_ref):
    xn = x_ref[...] + g_ref[...] * _dot(a_ref[...], w_ref[...])
    xo_ref[...] = xn
    y = xn * lax.rsqrt(jnp.mean(xn * xn, axis=-1, keepdims=True) + EPS)
    ho_ref[...] = (y * (1.0 + sc_ref[...]) + sh_ref[...]).astype(ho_ref.dtype)


def _out_proj_norm(a, w, layer, x2, mod3, k_gate, k_shift, k_scale, seq, *, bm):
    M, K = a.shape
    N = w.shape[2]
    per_seq = seq // bm

    def mod(k):
        return pl.BlockSpec((None, 1, N), lambda i: (i // per_seq, 0, k))

    return pl.pallas_call(
        _out_norm_kernel,
        out_shape=(jax.ShapeDtypeStruct((M, N), F32), jax.ShapeDtypeStruct((M, N), BF16)),
        grid=(M // bm,),
        in_specs=[
            pl.BlockSpec((bm, K), lambda i: (i, 0)),
            pl.BlockSpec((None, K, N), lambda i: (layer, 0, 0), pipeline_mode=pl.Buffered(1)),
            pl.BlockSpec((bm, N), lambda i: (i, 0)),
            mod(k_gate), mod(k_shift), mod(k_scale),
        ],
        out_specs=(pl.BlockSpec((bm, N), lambda i: (i, 0)), pl.BlockSpec((bm, N), lambda i: (i, 0))),
        compiler_params=_params(("parallel",), V7X_VMEM_BYTES - 4 * 1024 * 1024),
        name="out_proj_norm",
    )(a, w, x2, mod3, mod3, mod3)


def _causal_conv3(p, halo, w):
    row = lax.broadcasted_iota(jnp.int32, p.shape, 0)
    s1 = pltpu.roll(p, 1, axis=0)
    s2 = pltpu.roll(p, 2, axis=0)
    h7 = halo[SUBLANE - 1:SUBLANE, :]
    h6 = halo[SUBLANE - 2:SUBLANE - 1, :]
    s1 = jnp.where(row == 0, h7, s1)
    s2 = jnp.where(row == 0, h6, jnp.where(row == 1, h7, s2))
    return w[0:1, :] * s2 + w[1:2, :] * s1 + w[2:3, :] * p


def _load_halo(halo_ref, j, first):
    h = halo_ref[j]
    return jnp.where(first, jnp.zeros_like(h), h)


def _sconv_in_kernel(a_ref, wb_ref, wc_ref, wx_ref, cw_ref, o_ref, halo_ref, *, per_seq):
    i = pl.program_id(0)
    j = pl.program_id(1)
    a = a_ref[...]
    p = _dot(a, wc_ref[...]) * _dot(a, wx_ref[...])
    halo = _load_halo(halo_ref, j, i % per_seq == 0)
    halo_ref[j] = p[p.shape[0] - SUBLANE:, :]
    o_ref[...] = (_dot(a, wb_ref[...]) * _causal_conv3(p, halo, cw_ref[...])).astype(o_ref.dtype)


def _sconv_in(a, w_in, layer, conv_w3, seq):
    M, D = a.shape
    bm = _tile(seq, 1024, SUBLANE)
    bn = _tile(D, 512, LANE)
    nb = D // bn
    return pl.pallas_call(
        functools.partial(_sconv_in_kernel, per_seq=seq // bm),
        out_shape=jax.ShapeDtypeStruct((M, D), BF16),
        grid=(M // bm, nb),
        in_specs=[
            pl.BlockSpec((bm, D), lambda i, j: (i, 0)),
            pl.BlockSpec((None, D, bn), lambda i, j: (layer, 0, j)),
            pl.BlockSpec((None, D, bn), lambda i, j: (layer, 0, nb + j)),
            pl.BlockSpec((None, D, bn), lambda i, j: (layer, 0, 2 * nb + j)),
            pl.BlockSpec((3, bn), lambda i, j: (0, j)),
        ],
        out_specs=pl.BlockSpec((bm, bn), lambda i, j: (i, j)),
        scratch_shapes=[pltpu.VMEM((nb, SUBLANE, bn), F32)],
        compiler_params=_params(("arbitrary", "arbitrary")),
        name="sconv_in",
    )(a, w_in, w_in, w_in, conv_w3)


def _ffn_up_kernel(a_ref, wa_ref, wu0_ref, wu1_ref, cw_ref, cb_ref, o_ref, halo_ref,
                   *, per_seq, ragged):
    i = pl.program_id(0)
    j = pl.program_id(1)
    bn = o_ref.shape[1]
    half = bn // 2

    def compute(width):
        a = a_ref[...]
        p = _dot(a, wa_ref[:, :width])
        halo = _load_halo(halo_ref, j, i % per_seq == 0)[:, :width]
        halo_ref[j, :, :width] = p[p.shape[0] - SUBLANE:, :]
        gate = _silu(_causal_conv3(p, halo, cw_ref[:, :width]) + cb_ref[:, :width])
        u = _dot(a, wu0_ref[...])
        if width == bn:
            u = jnp.concatenate([u, _dot(a, wu1_ref[...])], axis=1)
        o_ref[:, :width] = (gate * u).astype(o_ref.dtype)

    if ragged:
        last = pl.num_programs(1) - 1
        pl.when(j < last)(lambda: compute(bn))
        pl.when(j == last)(lambda: compute(half))
    else:
        compute(bn)


def _ffn_up(a, w_up, layer, conv_w3, conv_b, seq):
    M, D = a.shape
    F = w_up.shape[2] // 2
    bm = _tile(seq, 1024, SUBLANE)
    bn = FFN_BN
    half = bn // 2
    assert F % half == 0
    nb = pl.cdiv(F, bn)
    u0 = F // half
    last = 2 * F // half - 1
    return pl.pallas_call(
        functools.partial(_ffn_up_kernel, per_seq=seq // bm, ragged=F % bn != 0),
        out_shape=jax.ShapeDtypeStruct((M, F), BF16),
        grid=(M // bm, nb),
        in_specs=[
            pl.BlockSpec((bm, D), lambda i, j: (i, 0)),
            pl.BlockSpec((None, D, bn), lambda i, j: (layer, 0, j)),
            pl.BlockSpec((None, D, half), lambda i, j: (layer, 0, u0 + 2 * j)),
            pl.BlockSpec((None, D, half), lambda i, j: (layer, 0, jnp.minimum(u0 + 2 * j + 1, last))),
            pl.BlockSpec((3, bn), lambda i, j: (0, j)),
            pl.BlockSpec((1, bn), lambda i, j: (0, j)),
        ],
        out_specs=pl.BlockSpec((bm, bn), lambda i, j: (i, j)),
        scratch_shapes=[pltpu.VMEM((nb, SUBLANE, bn), F32)],
        compiler_params=_params(("arbitrary", "arbitrary")),
        name="ffn_up",
    )(a, w_up, w_up, w_up, conv_w3, conv_b)


N_FINE = 3
LOG_SPLIT = 2


def _hgrn_constants(C, W):
    u = np.arange(C)[None, :]
    t = np.arange(C)[:, None]
    bands = [u <= t]
    masks = []
    for l in range(N_FINE):
        h = 1 << l
        m = (t // (2 * h)) * (2 * h) + h
        upper = t >= m
        bands.append(np.where(upper, (u > m) & (u <= t), (u > t) & (u <= m)))
        masks.append(np.broadcast_to(upper[:SUBLANE], (SUBLANE, W)))
    d = np.concatenate(bands, axis=0).astype(np.float32)
    dmat = np.concatenate([d] * LOG_SPLIT, axis=1)
    rowmask = np.concatenate(masks, axis=0).astype(np.float32)
    s = np.arange(2 * C)[None, :] % C
    x = t ^ s
    lev = np.where(t == s, 0, np.where(t > s, np.floor(np.log2(np.maximum(x, 1))) + 1, -1))
    hh = np.arange(W) // HEAD
    mean = (hh[:, None] == hh[None, :]).astype(np.float32) / HEAD
    return (jnp.asarray(dmat, BF16), jnp.asarray(rowmask, F32),
            jnp.asarray(lev, jnp.int32), jnp.asarray(mean, BF16))


def _hgrn_kernel(q_ref, f_ref, v_ref, g_ref, lbl_ref, gn_ref, dmat_ref, rmask_ref, lev_ref,
                 mean_ref, o_ref, st_ref, *, n_chunks, slot, group):
    C = CHUNK
    W = 2 * HEAD
    n_lvl = int(np.log2(C))

    @pl.when(pl.program_id(2) == 0)
    def _():
        st_ref[...] = jnp.zeros_like(st_ref)

    rows = [lbl_ref[k:k + 1, :] for k in range(lbl_ref.shape[0])]
    mx = functools.reduce(jnp.maximum, rows)
    ex = [jnp.exp(r - mx) for r in rows]
    lb = functools.reduce(jnp.add, ex[:slot + 1]) / functools.reduce(jnp.add, ex)
    gn = gn_ref[...]
    zc = jnp.zeros((C, HEAD), BF16)
    zs = jnp.zeros((HEAD, HEAD), BF16)

    def by_head(x):
        return jnp.concatenate(
            [jnp.concatenate([x[:, :HEAD], zc], axis=1),
             jnp.concatenate([zc, x[:, HEAD:]], axis=1)], axis=0)

    G = group
    R = G * C

    def per_chunk(x):
        return [x[c * C:(c + 1) * C] for c in range(G)]

    def body(gi, carry):
        rs = pl.ds(pl.multiple_of(gi * R, R), R)
        q_all = q_ref[rs, :]
        f_all = lb + (1.0 - lb) * jax.nn.sigmoid(f_ref[rs, :])
        k_all = 1.0 - f_all
        lg = jnp.log(f_all) * LOG2E
        pieces, rem = [], lg
        for _ in range(LOG_SPLIT - 1):
            pieces.append(rem.astype(BF16))
            rem = rem - pieces[-1].astype(F32)
        pieces.append(rem.astype(BF16))
        q, k = per_chunk(q_all), per_chunk(k_all)
        vb = per_chunk(v_ref[rs, :].astype(BF16))
        dmat = dmat_ref[...]
        args = [_dot(dmat, jnp.concatenate(list(xs), axis=0))
                for xs in zip(*[per_chunk(x) for x in pieces])]
        b = [x[0:C] for x in args]
        b_last = [x[C - 1:C, :] for x in b]
        e_fine = [jnp.exp2(x[C:]) for x in args]
        lev = lev_ref[...]

        a = [jnp.where(lev == 0, _dot_nt(q[c].astype(BF16), by_head(k[c].astype(BF16))), 0.0)
             for c in range(G)]
        for l in range(N_FINE):
            up = jnp.tile(rmask_ref[l * SUBLANE:(l + 1) * SUBLANE, :], (C // SUBLANE, 1)) != 0.0
            for c in range(G):
                z = (jnp.where(up, q[c], k[c]) * e_fine[c][l * C:(l + 1) * C]).astype(BF16)
                a[c] = jnp.where(lev == l + 1, _dot_nt(z, by_head(z)), a[c])
        for l in range(N_FINE, n_lvl):
            h = 1 << l
            zh = jnp.zeros((h, W), BF16)
            for c in range(G):
                qe, ke = [], []
                for base in range(0, C, 2 * h):
                    m = base + h
                    bm = b[c][m:m + 1, :]
                    qe += [zh, (q[c][m:m + h] * jnp.exp2(b[c][m:m + h] - bm)).astype(BF16)]
                    ke += [(k[c][base:m] * jnp.exp2(bm - b[c][base:m])).astype(BF16), zh]
                p = _dot_nt(jnp.concatenate(qe, axis=0), by_head(jnp.concatenate(ke, axis=0)))
                a[c] = jnp.where(lev == l + 1, p, a[c])
        o_intra = [_dot(a[c].astype(BF16), by_head(vb[c])) for c in range(G)]

        qh = [(q[c] * jnp.exp2(b[c])).astype(BF16) for c in range(G)]
        kh = [(k[c] * jnp.exp2(b_last[c] - b[c])).astype(BF16) for c in range(G)]
        upd0 = [_dot_tn(vb[c][:, :HEAD], kh[c][:, :HEAD]) for c in range(G)]
        upd1 = [_dot_tn(vb[c][:, HEAD:], kh[c][:, HEAD:]) for c in range(G)]
        d_s = [jnp.exp2(x) for x in b_last]
        s0 = st_ref[0]
        s1 = st_ref[1]
        o = []
        for c in range(G):
            s_bd = jnp.concatenate(
                [jnp.concatenate([s0.astype(BF16), zs], axis=1),
                 jnp.concatenate([zs, s1.astype(BF16)], axis=1)], axis=0)
            o.append(_dot_nt(qh[c], s_bd) + o_intra[c])
            s0 = s0 * d_s[c][:, :HEAD] + upd0[c]
            s1 = s1 * d_s[c][:, HEAD:] + upd1[c]
        st_ref[0] = s0
        st_ref[1] = s1

        o_all = jnp.concatenate(o, axis=0)
        ms = _dot((o_all * o_all).astype(BF16), mean_ref[...])
        g = g_ref[rs, :]
        o_ref[rs, :] = (o_all * lax.rsqrt(ms + EPS) * gn * _silu(g)).astype(o_ref.dtype)
        return carry

    lax.fori_loop(0, n_chunks // G, body, 0)


def _hgrn_recurrence(proj, lb_logits, g_norm, batch, seq, slot):
    M, D4 = proj.shape
    D = D4 // 4
    W = 2 * HEAD
    npair = D // W
    tb = _tile(seq, 1024, CHUNK)
    nt = seq // tb
    group = _tile(tb // CHUNK, 8, 1)
    consts = _hgrn_constants(CHUNK, W)

    def col(part):
        return pl.BlockSpec((tb, W), lambda b, p, t: (b * nt + t, part * npair + p))

    def whole(arr):
        return pl.BlockSpec(arr.shape, lambda b, p, t: (0, 0))

    return pl.pallas_call(
        functools.partial(_hgrn_kernel, n_chunks=tb // CHUNK, slot=slot, group=group),
        out_shape=jax.ShapeDtypeStruct((M, D), BF16),
        grid=(batch, npair, nt),
        in_specs=[col(0), col(1), col(2), col(3),
                  pl.BlockSpec((lb_logits.shape[0], W), lambda b, p, t: (0, p)),
                  pl.BlockSpec((1, W), lambda b, p, t: (0, p))] + [whole(c) for c in consts],
        out_specs=pl.BlockSpec((tb, W), lambda b, p, t: (b * nt + t, p)),
        scratch_shapes=[pltpu.VMEM((2, HEAD, HEAD), F32)],
        compiler_params=_params(("parallel", "parallel", "arbitrary")),
        name="hgrn_recurrence",
    )(proj, proj, proj, proj, lb_logits, g_norm.reshape(1, D), *consts)


def kernel(x, c, w_cond, b_cond, ada_w, ada_b, lb_logits, hgrn_w_in, hgrn_norm, hgrn_w_out,
           sconv_w_in, sconv_conv_w, sconv_w_out, ffn_w_up, ffn_conv_w, ffn_conv_b,
           ffn_w_down, final_norm):
    B, T, D = x.shape
    depth = ada_w.shape[0]
    F = ffn_conv_b.shape[1]
    M = B * T
    n_mixers = 2

    c_emb = _cond_matmul(c, w_cond[None], b_cond[None], act=True)[0]
    mod = _cond_matmul(c_emb, ada_w, ada_b, act=False)

    w_hin, w_hout = hgrn_w_in.astype(BF16), hgrn_w_out.astype(BF16)
    w_sin, w_sout = sconv_w_in.astype(BF16), sconv_w_out.astype(BF16)
    w_up, w_down = ffn_w_up.astype(BF16), ffn_w_down.astype(BF16)

    x2 = x.reshape(M, D)
    for i in range(depth):
        mod3 = mod[i].reshape(B, 1, N_MOD * D)
        j = i // n_mixers
        hm = _prenorm(x2.reshape(B, T, D), mod3, 0, 1).reshape(M, D)
        if i % n_mixers == 0:
            proj = _matmul(hm, w_hin, j, F32)
            y = _hgrn_recurrence(proj, lb_logits, hgrn_norm[j], B, T, slot=i)
            x2, hf = _out_proj_norm(y, w_hout, j, x2, mod3, 2, 3, 4, T, bm=OUT_BM)
        else:
            y = _sconv_in(hm, w_sin, j, sconv_conv_w[j].T, T)
            x2, hf = _out_proj_norm(y, w_sout, j, x2, mod3, 2, 3, 4, T, bm=OUT_BM)

        h = _ffn_up(hf, w_up, i, ffn_conv_w[i].T, ffn_conv_b[i].reshape(1, F), T)
        x2 = _matmul_residual(h, w_down, i, x2, mod3, 5, T, blk=512)
    return _final_norm(x2, final_norm).reshape(B, T, D)
```

```python
import functools

import numpy as np
import jax
import jax.numpy as jnp
from jax import lax
from jax.experimental import pallas as pl
from jax.experimental.pallas import tpu as pltpu

EPS = 1e-6
CHUNK = 64
HEAD = 128
N_MOD = 6
LOG2E = 1.4426950408889634
F32 = jnp.float32
BF16 = jnp.bfloat16

V7X_VMEM_BYTES = 64 * 1024 * 1024
VMEM_LIMIT = V7X_VMEM_BYTES - 8 * 1024 * 1024
LANE = 128
SUBLANE = 8
BF16_SUBLANE = 16
FFN_BN = 512
OUT_BM = 256


def _tile(n, pref, align):
    t = min(pref, n)
    t -= t % align
    while t >= align:
        if n % t == 0:
            return t
        t -= align
    return n


def _params(sem, vmem=VMEM_LIMIT):
    return pltpu.CompilerParams(dimension_semantics=sem, vmem_limit_bytes=vmem)


def _dot(a, b):
    return jnp.dot(a, b, preferred_element_type=F32)


def _dot_nt(a, b):
    return lax.dot_general(a, b, (((1,), (1,)), ((), ())), preferred_element_type=F32)


def _dot_tn(a, b):
    return lax.dot_general(a, b, (((0,), (0,)), ((), ())), preferred_element_type=F32)


def _silu(x):
    return x * jax.nn.sigmoid(x)


def _cond_kernel(c_ref, w_ref, b_ref, o_ref, *, act):
    acc = _dot(c_ref[...].astype(BF16), w_ref[...].astype(BF16)) + b_ref[...]
    o_ref[...] = _silu(acc) if act else acc


def _cond_matmul(c, w, b, *, act):
    L, K, N = w.shape
    B = c.shape[0]
    bn = _tile(N, 512, LANE)
    return pl.pallas_call(
        functools.partial(_cond_kernel, act=act),
        out_shape=jax.ShapeDtypeStruct((L, B, N), F32),
        grid=(L, N // bn),
        in_specs=[
            pl.BlockSpec((B, K), lambda l, j: (0, 0)),
            pl.BlockSpec((None, K, bn), lambda l, j: (l, 0, j)),
            pl.BlockSpec((None, 1, bn), lambda l, j: (l, 0, j)),
        ],
        out_specs=pl.BlockSpec((None, B, bn), lambda l, j: (l, 0, j)),
        compiler_params=_params(("parallel", "parallel")),
        name="cond_matmul",
    )(c, w, b.reshape(L, 1, N))


def _prenorm_kernel(x_ref, sh_ref, sc_ref, o_ref):
    x = x_ref[...]
    y = x * lax.rsqrt(jnp.mean(x * x, axis=-1, keepdims=True) + EPS)
    o_ref[...] = (y * (1.0 + sc_ref[...]) + sh_ref[...]).astype(o_ref.dtype)


def _prenorm(x3, mod3, k_shift, k_scale):
    B, T, D = x3.shape
    bt = _tile(T, 512, SUBLANE)
    return pl.pallas_call(
        _prenorm_kernel,
        out_shape=jax.ShapeDtypeStruct((B, T, D), BF16),
        grid=(B, T // bt),
        in_specs=[
            pl.BlockSpec((None, bt, D), lambda b, t: (b, t, 0)),
            pl.BlockSpec((None, 1, D), lambda b, t: (b, 0, k_shift)),
            pl.BlockSpec((None, 1, D), lambda b, t: (b, 0, k_scale)),
        ],
        out_specs=pl.BlockSpec((None, bt, D), lambda b, t: (b, t, 0)),
        compiler_params=_params(("parallel", "parallel")),
        name="prenorm",
    )(x3, mod3, mod3)


def _final_norm_kernel(x_ref, w_ref, o_ref):
    x = x_ref[...]
    o_ref[...] = x * lax.rsqrt(jnp.mean(x * x, axis=-1, keepdims=True) + EPS) * w_ref[...]


def _final_norm(x2, w):
    M, D = x2.shape
    bt = _tile(M, 512, SUBLANE)
    return pl.pallas_call(
        _final_norm_kernel,
        out_shape=jax.ShapeDtypeStruct((M, D), F32),
        grid=(M // bt,),
        in_specs=[pl.BlockSpec((bt, D), lambda i: (i, 0)),
                  pl.BlockSpec((1, D), lambda i: (0, 0))],
        out_specs=pl.BlockSpec((bt, D), lambda i: (i, 0)),
        compiler_params=_params(("parallel",)),
        name="final_norm",
    )(x2, w.reshape(1, D))


def _cast_split(shape, steps):
    rows, cols = shape
    nc = 1
    while nc <= steps:
        nr, rem = divmod(steps, nc)
        if (rem == 0 and rows % nr == 0 and cols % nc == 0
                and (rows // nr) % BF16_SUBLANE == 0 and (cols // nc) % LANE == 0):
            return (rows // nr, cols // nc), nc
        nc *= 2
    return None


def _side_casts(side, grid):
    flat = [s.reshape(-1, s.shape[-1]) for s in side]
    splits = [_cast_split(f.shape, int(np.prod(grid))) for f in flat]
    if any(sp is None for sp in splits):
        return None

    def spec(blk, nc):
        def index(*idx):
            step = functools.reduce(lambda acc, t: acc * t[1] + t[0], zip(idx, grid), 0)
            return step // nc, step % nc
        return pl.BlockSpec(blk, index)

    return (flat, [spec(*sp) for sp in splits],
            [jax.ShapeDtypeStruct(f.shape, BF16) for f in flat])


def _cast_blocks(srcs, dsts):
    for src, dst in zip(srcs, dsts):
        dst[...] = src[...].astype(dst.dtype)


def _mm_cast_kernel(a_ref, w_ref, *refs, n_side):
    o_ref = refs[n_side]
    o_ref[...] = _dot(a_ref[...], w_ref[...]).astype(o_ref.dtype)
    _cast_blocks(refs[:n_side], refs[n_side + 1:])


def _matmul(a, w, layer, out_dtype, side=()):
    M, K = a.shape
    N = w.shape[2]
    bm = _tile(M, 1024, SUBLANE)
    bn = _tile(N, 1024, LANE)
    grid = (M // bm, N // bn)
    casts = _side_casts(side, grid)
    if casts is None:
        return _matmul(a, w, layer, out_dtype)[0], [s.astype(BF16) for s in side]
    flat, side_specs, side_shapes = casts
    outs = pl.pallas_call(
        functools.partial(_mm_cast_kernel, n_side=len(flat)),
        out_shape=[jax.ShapeDtypeStruct((M, N), out_dtype)] + side_shapes,
        grid=grid,
        in_specs=[pl.BlockSpec((bm, K), lambda i, j: (i, 0)),
                  pl.BlockSpec((None, K, bn), lambda i, j: (layer, 0, j))] + side_specs,
        out_specs=[pl.BlockSpec((bm, bn), lambda i, j: (i, j))] + side_specs,
        compiler_params=_params(("parallel", "parallel"), V7X_VMEM_BYTES - 4 * 1024 * 1024),
        name="matmul",
    )(a, w, *flat)
    return outs[0], [o.reshape(s.shape) for o, s in zip(outs[1:], side)]


def _mm_res_kernel(a_ref, w_ref, x_ref, g_ref, o_ref):
    o_ref[...] = x_ref[...] + g_ref[...] * _dot(a_ref[...], w_ref[...])


def _matmul_residual(a, w, layer, x2, mod3, k_gate, seq, *, blk):
    M, K = a.shape
    N = w.shape[2]
    bm = _tile(seq, blk, SUBLANE)
    bn = _tile(N, blk, LANE)
    nb = N // bn
    per_seq = seq // bm
    return pl.pallas_call(
        _mm_res_kernel,
        out_shape=jax.ShapeDtypeStruct((M, N), F32),
        grid=(M // bm, nb),
        in_specs=[
            pl.BlockSpec((bm, K), lambda i, j: (i, 0)),
            pl.BlockSpec((None, K, bn), lambda i, j: (layer, 0, j)),
            pl.BlockSpec((bm, bn), lambda i, j: (i, j)),
            pl.BlockSpec((None, 1, bn), lambda i, j: (i // per_seq, 0, k_gate * nb + j)),
        ],
        out_specs=pl.BlockSpec((bm, bn), lambda i, j: (i, j)),
        compiler_params=_params(("parallel", "parallel")),
        name="matmul_residual",
    )(a, w, x2, mod3)


def _out_norm_kernel(a_ref, w_ref, x_ref, g_ref, sh_ref, sc_ref, xo_ref, ho_ref):
    xn = x_ref[...] + g_ref[...] * _dot(a_ref[...], w_ref[...])
    xo_ref[...] = xn
    y = xn * lax.rsqrt(jnp.mean(xn * xn, axis=-1, keepdims=True) + EPS)
    ho_ref[...] = (y * (1.0 + sc_ref[...]) + sh_ref[...]).astype(ho_ref.dtype)


def _out_proj_norm(a, w, layer, x2, mod3, k_gate, k_shift, k_scale, seq, *, bm):
    M, K = a.shape
    N = w.shape[2]
    per_seq = seq // bm

    def mod(k):
        return pl.BlockSpec((None, 1, N), lambda i: (i // per_seq, 0, k))

    return pl.pallas_call(
        _out_norm_kernel,
        out_shape=(jax.ShapeDtypeStruct((M, N), F32), jax.ShapeDtypeStruct((M, N), BF16)),
        grid=(M // bm,),
        in_specs=[
            pl.BlockSpec((bm, K), lambda i: (i, 0)),
            pl.BlockSpec((None, K, N), lambda i: (layer, 0, 0), pipeline_mode=pl.Buffered(1)),
            pl.BlockSpec((bm, N), lambda i: (i, 0)),
            mod(k_gate), mod(k_shift), mod(k_scale),
        ],
        out_specs=(pl.BlockSpec((bm, N), lambda i: (i, 0)), pl.BlockSpec((bm, N), lambda i: (i, 0))),
        compiler_params=_params(("parallel",), V7X_VMEM_BYTES - 4 * 1024 * 1024),
        name="out_proj_norm",
    )(a, w, x2, mod3, mod3, mod3)


def _causal_conv3(p, halo, w):
    row = lax.broadcasted_iota(jnp.int32, p.shape, 0)
    s1 = pltpu.roll(p, 1, axis=0)
    s2 = pltpu.roll(p, 2, axis=0)
    h7 = halo[SUBLANE - 1:SUBLANE, :]
    h6 = halo[SUBLANE - 2:SUBLANE - 1, :]
    s1 = jnp.where(row == 0, h7, s1)
    s2 = jnp.where(row == 0, h6, jnp.where(row == 1, h7, s2))
    return w[0:1, :] * s2 + w[1:2, :] * s1 + w[2:3, :] * p


def _load_halo(halo_ref, j, first):
    h = halo_ref[j]
    return jnp.where(first, jnp.zeros_like(h), h)


def _sconv_in_kernel(a_ref, wb_ref, wc_ref, wx_ref, cw_ref, o_ref, halo_ref, *, per_seq):
    i = pl.program_id(0)
    j = pl.program_id(1)
    a = a_ref[...]
    p = _dot(a, wc_ref[...]) * _dot(a, wx_ref[...])
    halo = _load_halo(halo_ref, j, i % per_seq == 0)
    halo_ref[j] = p[p.shape[0] - SUBLANE:, :]
    o_ref[...] = (_dot(a, wb_ref[...]) * _causal_conv3(p, halo, cw_ref[...])).astype(o_ref.dtype)


def _sconv_in(a, w_in, layer, conv_w3, seq):
    M, D = a.shape
    bm = _tile(seq, 1024, SUBLANE)
    bn = _tile(D, 512, LANE)
    nb = D // bn
    return pl.pallas_call(
        functools.partial(_sconv_in_kernel, per_seq=seq // bm),
        out_shape=jax.ShapeDtypeStruct((M, D), BF16),
        grid=(M // bm, nb),
        in_specs=[
            pl.BlockSpec((bm, D), lambda i, j: (i, 0)),
            pl.BlockSpec((None, D, bn), lambda i, j: (layer, 0, j)),
            pl.BlockSpec((None, D, bn), lambda i, j: (layer, 0, nb + j)),
            pl.BlockSpec((None, D, bn), lambda i, j: (layer, 0, 2 * nb + j)),
            pl.BlockSpec((3, bn), lambda i, j: (0, j)),
        ],
        out_specs=pl.BlockSpec((bm, bn), lambda i, j: (i, j)),
        scratch_shapes=[pltpu.VMEM((nb, SUBLANE, bn), F32)],
        compiler_params=_params(("arbitrary", "arbitrary")),
        name="sconv_in",
    )(a, w_in, w_in, w_in, conv_w3)


def _ffn_up_kernel(a_ref, wa_ref, wu0_ref, wu1_ref, cw_ref, cb_ref, o_ref, halo_ref,
                   *, per_seq, ragged):
    i = pl.program_id(0)
    j = pl.program_id(1)
    bn = o_ref.shape[1]
    half = bn // 2

    def compute(width):
        a = a_ref[...]
        p = _dot(a, wa_ref[:, :width])
        halo = _load_halo(halo_ref, j, i % per_seq == 0)[:, :width]
        halo_ref[j, :, :width] = p[p.shape[0] - SUBLANE:, :]
        gate = _silu(_causal_conv3(p, halo, cw_ref[:, :width]) + cb_ref[:, :width])
        u = _dot(a, wu0_ref[...])
        if width == bn:
            u = jnp.concatenate([u, _dot(a, wu1_ref[...])], axis=1)
        o_ref[:, :width] = (gate * u).astype(o_ref.dtype)

    if ragged:
        last = pl.num_programs(1) - 1
        pl.when(j < last)(lambda: compute(bn))
        pl.when(j == last)(lambda: compute(half))
    else:
        compute(bn)


def _ffn_up(a, w_up, layer, conv_w3, conv_b, seq):
    M, D = a.shape
    F = w_up.shape[2] // 2
    bm = _tile(seq, 1024, SUBLANE)
    bn = FFN_BN
    half = bn // 2
    assert F % half == 0
    nb = pl.cdiv(F, bn)
    u0 = F // half
    last = 2 * F // half - 1
    return pl.pallas_call(
        functools.partial(_ffn_up_kernel, per_seq=seq // bm, ragged=F % bn != 0),
        out_shape=jax.ShapeDtypeStruct((M, F), BF16),
        grid=(M // bm, nb),
        in_specs=[
            pl.BlockSpec((bm, D), lambda i, j: (i, 0)),
            pl.BlockSpec((None, D, bn), lambda i, j: (layer, 0, j)),
            pl.BlockSpec((None, D, half), lambda i, j: (layer, 0, u0 + 2 * j)),
            pl.BlockSpec((None, D, half), lambda i, j: (layer, 0, jnp.minimum(u0 + 2 * j + 1, last))),
            pl.BlockSpec((3, bn), lambda i, j: (0, j)),
            pl.BlockSpec((1, bn), lambda i, j: (0, j)),
        ],
        out_specs=pl.BlockSpec((bm, bn), lambda i, j: (i, j)),
        scratch_shapes=[pltpu.VMEM((nb, SUBLANE, bn), F32)],
        compiler_params=_params(("arbitrary", "arbitrary")),
        name="ffn_up",
    )(a, w_up, w_up, w_up, conv_w3, conv_b)


N_FINE = 3
LOG_SPLIT = 2


def _hgrn_constants(C, W):
    u = np.arange(C)[None, :]
    t = np.arange(C)[:, None]
    bands = [u <= t]
    masks = []
    for l in range(N_FINE):
        h = 1 << l
        m = (t // (2 * h)) * (2 * h) + h
        upper = t >= m
        bands.append(np.where(upper, (u > m) & (u <= t), (u > t) & (u <= m)))
        masks.append(np.broadcast_to(upper[:SUBLANE], (SUBLANE, W)))
    d = np.concatenate(bands, axis=0).astype(np.float32)
    dmat = np.concatenate([d] * LOG_SPLIT, axis=1)
    rowmask = np.concatenate(masks, axis=0).astype(np.float32)
    s = np.arange(2 * C)[None, :] % C
    x = t ^ s
    lev = np.where(t == s, 0, np.where(t > s, np.floor(np.log2(np.maximum(x, 1))) + 1, -1))
    hh = np.arange(W) // HEAD
    mean = (hh[:, None] == hh[None, :]).astype(np.float32) / HEAD
    return (jnp.asarray(dmat, BF16), jnp.asarray(rowmask, F32),
            jnp.asarray(lev, jnp.int32), jnp.asarray(mean, BF16))


def _hgrn_kernel(q_ref, f_ref, v_ref, g_ref, lbl_ref, gn_ref, dmat_ref, rmask_ref, lev_ref,
                 mean_ref, *refs, n_chunks, slot, group, n_side):
    o_ref, st_ref = refs[n_side], refs[-1]
    _cast_blocks(refs[:n_side], refs[n_side + 1:-1])
    C = CHUNK
    W = 2 * HEAD
    n_lvl = int(np.log2(C))

    @pl.when(pl.program_id(2) == 0)
    def _():
        st_ref[...] = jnp.zeros_like(st_ref)

    rows = [lbl_ref[k:k + 1, :] for k in range(lbl_ref.shape[0])]
    mx = functools.reduce(jnp.maximum, rows)
    ex = [jnp.exp(r - mx) for r in rows]
    lb = functools.reduce(jnp.add, ex[:slot + 1]) / functools.reduce(jnp.add, ex)
    gn = gn_ref[...]
    zc = jnp.zeros((C, HEAD), BF16)
    zs = jnp.zeros((HEAD, HEAD), BF16)

    def by_head(x):
        return jnp.concatenate(
            [jnp.concatenate([x[:, :HEAD], zc], axis=1),
             jnp.concatenate([zc, x[:, HEAD:]], axis=1)], axis=0)

    G = group
    R = G * C

    def per_chunk(x):
        return [x[c * C:(c + 1) * C] for c in range(G)]

    def body(gi, carry):
        rs = pl.ds(pl.multiple_of(gi * R, R), R)
        q_all = q_ref[rs, :]
        f_all = lb + (1.0 - lb) * jax.nn.sigmoid(f_ref[rs, :])
        k_all = 1.0 - f_all
        lg = jnp.log(f_all) * LOG2E
        pieces, rem = [], lg
        for _ in range(LOG_SPLIT - 1):
            pieces.append(rem.astype(BF16))
            rem = rem - pieces[-1].astype(F32)
        pieces.append(rem.astype(BF16))
        q, k = per_chunk(q_all), per_chunk(k_all)
        vb = per_chunk(v_ref[rs, :].astype(BF16))
        dmat = dmat_ref[...]
        args = [_dot(dmat, jnp.concatenate(list(xs), axis=0))
                for xs in zip(*[per_chunk(x) for x in pieces])]
        b = [x[0:C] for x in args]
        b_last = [x[C - 1:C, :] for x in b]
        e_fine = [jnp.exp2(x[C:]) for x in args]
        lev = lev_ref[...]

        a = [jnp.where(lev == 0, _dot_nt(q[c].astype(BF16), by_head(k[c].astype(BF16))), 0.0)
             for c in range(G)]
        for l in range(N_FINE):
            up = jnp.tile(rmask_ref[l * SUBLANE:(l + 1) * SUBLANE, :], (C // SUBLANE, 1)) != 0.0
            for c in range(G):
                z = (jnp.where(up, q[c], k[c]) * e_fine[c][l * C:(l + 1) * C]).astype(BF16)
                a[c] = jnp.where(lev == l + 1, _dot_nt(z, by_head(z)), a[c])
        for l in range(N_FINE, n_lvl):
            h = 1 << l
            zh = jnp.zeros((h, W), BF16)
            for c in range(G):
                qe, ke = [], []
                for base in range(0, C, 2 * h):
                    m = base + h
                    bm = b[c][m:m + 1, :]
                    qe += [zh, (q[c][m:m + h] * jnp.exp2(b[c][m:m + h] - bm)).astype(BF16)]
                    ke += [(k[c][base:m] * jnp.exp2(bm - b[c][base:m])).astype(BF16), zh]
                p = _dot_nt(jnp.concatenate(qe, axis=0), by_head(jnp.concatenate(ke, axis=0)))
                a[c] = jnp.where(lev == l + 1, p, a[c])
        o_intra = [_dot(a[c].astype(BF16), by_head(vb[c])) for c in range(G)]

        qh = [(q[c] * jnp.exp2(b[c])).astype(BF16) for c in range(G)]
        kh = [(k[c] * jnp.exp2(b_last[c] - b[c])).astype(BF16) for c in range(G)]
        upd0 = [_dot_tn(vb[c][:, :HEAD], kh[c][:, :HEAD]) for c in range(G)]
        upd1 = [_dot_tn(vb[c][:, HEAD:], kh[c][:, HEAD:]) for c in range(G)]
        d_s = [jnp.exp2(x) for x in b_last]
        s0 = st_ref[0]
        s1 = st_ref[1]
        o = []
        for c in range(G):
            s_bd = jnp.concatenate(
                [jnp.concatenate([s0.astype(BF16), zs], axis=1),
                 jnp.concatenate([zs, s1.astype(BF16)], axis=1)], axis=0)
            o.append(_dot_nt(qh[c], s_bd) + o_intra[c])
            s0 = s0 * d_s[c][:, :HEAD] + upd0[c]
            s1 = s1 * d_s[c][:, HEAD:] + upd1[c]
        st_ref[0] = s0
        st_ref[1] = s1

        o_all = jnp.concatenate(o, axis=0)
        ms = _dot((o_all * o_all).astype(BF16), mean_ref[...])
        g = g_ref[rs, :]
        o_ref[rs, :] = (o_all * lax.rsqrt(ms + EPS) * gn * _silu(g)).astype(o_ref.dtype)
        return carry

    lax.fori_loop(0, n_chunks // G, body, 0)


def _hgrn_recurrence(proj, lb_logits, g_norm, batch, seq, slot, side=()):
    M, D4 = proj.shape
    D = D4 // 4
    W = 2 * HEAD
    npair = D // W
    tb = _tile(seq, 1024, CHUNK)
    nt = seq // tb
    group = _tile(tb // CHUNK, 8, 1)
    consts = _hgrn_constants(CHUNK, W)

    def col(part):
        return pl.BlockSpec((tb, W), lambda b, p, t: (b * nt + t, part * npair + p))

    def whole(arr):
        return pl.BlockSpec(arr.shape, lambda b, p, t: (0, 0))

    grid = (batch, npair, nt)
    casts = _side_casts(side, grid)
    if casts is None:
        return (_hgrn_recurrence(proj, lb_logits, g_norm, batch, seq, slot)[0],
                [s.astype(BF16) for s in side])
    flat, side_specs, side_shapes = casts
    outs = pl.pallas_call(
        functools.partial(_hgrn_kernel, n_chunks=tb // CHUNK, slot=slot, group=group,
                          n_side=len(flat)),
        out_shape=[jax.ShapeDtypeStruct((M, D), BF16)] + side_shapes,
        grid=grid,
        in_specs=[col(0), col(1), col(2), col(3),
                  pl.BlockSpec((lb_logits.shape[0], W), lambda b, p, t: (0, p)),
                  pl.BlockSpec((1, W), lambda b, p, t: (0, p))]
        + [whole(c) for c in consts] + side_specs,
        out_specs=[pl.BlockSpec((tb, W), lambda b, p, t: (b * nt + t, p))] + side_specs,
        scratch_shapes=[pltpu.VMEM((2, HEAD, HEAD), F32)],
        compiler_params=_params(("parallel", "parallel", "arbitrary")),
        name="hgrn_recurrence",
    )(proj, proj, proj, proj, lb_logits, g_norm.reshape(1, D), *consts, *flat)
    return outs[0], [o.reshape(s.shape) for o, s in zip(outs[1:], side)]


def kernel(x, c, w_cond, b_cond, ada_w, ada_b, lb_logits, hgrn_w_in, hgrn_norm, hgrn_w_out,
           sconv_w_in, sconv_conv_w, sconv_w_out, ffn_w_up, ffn_conv_w, ffn_conv_b,
           ffn_w_down, final_norm):
    B, T, D = x.shape
    depth = ada_w.shape[0]
    F = ffn_conv_b.shape[1]
    M = B * T
    n_mixers = 2

    c_emb = _cond_matmul(c, w_cond[None], b_cond[None], act=True)[0]
    mod = _cond_matmul(c_emb, ada_w, ada_b, act=False)

    w_hin = hgrn_w_in.astype(BF16)

    x2 = x.reshape(M, D)
    for i in range(depth):
        mod3 = mod[i].reshape(B, 1, N_MOD * D)
        j = i // n_mixers
        hm = _prenorm(x2.reshape(B, T, D), mod3, 0, 1).reshape(M, D)
        if i % n_mixers == 0:
            first = i == 0
            proj, copies = _matmul(hm, w_hin, j, F32, side=(hgrn_w_out, ffn_w_up) if first else ())
            if first:
                w_hout, w_up = copies
            y, copies = _hgrn_recurrence(proj, lb_logits, hgrn_norm[j], B, T, slot=i,
                                         side=(sconv_w_in, sconv_w_out, ffn_w_down) if first else ())
            if first:
                w_sin, w_sout, w_down = copies
            x2, hf = _out_proj_norm(y, w_hout, j, x2, mod3, 2, 3, 4, T, bm=OUT_BM)
        else:
            y = _sconv_in(hm, w_sin, j, sconv_conv_w[j].T, T)
            x2, hf = _out_proj_norm(y, w_sout, j, x2, mod3, 2, 3, 4, T, bm=OUT_BM)

        h = _ffn_up(hf, w_up, i, ffn_conv_w[i].T, ffn_conv_b[i].reshape(1, F), T)
        x2 = _matmul_residual(h, w_down, i, x2, mod3, 5, T, blk=512)
    return _final_norm(x2, final_norm).reshape(B, T, D)
```

```python
import functools

import numpy as np
import jax
import jax.numpy as jnp
from jax import lax
from jax.experimental import pallas as pl
from jax.experimental.pallas import tpu as pltpu

EPS = 1e-6
CHUNK = 64
HEAD = 128
N_MOD = 6
LOG2E = 1.4426950408889634
F32 = jnp.float32
BF16 = jnp.bfloat16

V7X_VMEM_BYTES = 64 * 1024 * 1024
VMEM_LIMIT = V7X_VMEM_BYTES - 8 * 1024 * 1024
LANE = 128
SUBLANE = 8
BF16_SUBLANE = 16
FFN_BN = 512
OUT_BM = 256


def _tile(n, pref, align):
    t = min(pref, n)
    t -= t % align
    while t >= align:
        if n % t == 0:
            return t
        t -= align
    return n


def _params(sem, vmem=VMEM_LIMIT):
    return pltpu.CompilerParams(dimension_semantics=sem, vmem_limit_bytes=vmem)


def _dot(a, b):
    return jnp.dot(a, b, preferred_element_type=F32)


def _dot_nt(a, b):
    return lax.dot_general(a, b, (((1,), (1,)), ((), ())), preferred_element_type=F32)


def _dot_tn(a, b):
    return lax.dot_general(a, b, (((0,), (0,)), ((), ())), preferred_element_type=F32)


def _silu(x):
    return x * jax.nn.sigmoid(x)


def _cond_kernel(c_ref, w_ref, b_ref, o_ref, *, act):
    acc = _dot(c_ref[...].astype(BF16), w_ref[...].astype(BF16)) + b_ref[...]
    o_ref[...] = _silu(acc) if act else acc


def _cond_matmul(c, w, b, *, act):
    L, K, N = w.shape
    B = c.shape[0]
    bn = _tile(N, 512, LANE)
    return pl.pallas_call(
        functools.partial(_cond_kernel, act=act),
        out_shape=jax.ShapeDtypeStruct((L, B, N), F32),
        grid=(L, N // bn),
        in_specs=[
            pl.BlockSpec((B, K), lambda l, j: (0, 0)),
            pl.BlockSpec((None, K, bn), lambda l, j: (l, 0, j)),
            pl.BlockSpec((None, 1, bn), lambda l, j: (l, 0, j)),
        ],
        out_specs=pl.BlockSpec((None, B, bn), lambda l, j: (l, 0, j)),
        compiler_params=_params(("parallel", "parallel")),
        name="cond_matmul",
    )(c, w, b.reshape(L, 1, N))


def _prenorm_kernel(x_ref, sh_ref, sc_ref, o_ref):
    x = x_ref[...]
    y = x * lax.rsqrt(jnp.mean(x * x, axis=-1, keepdims=True) + EPS)
    o_ref[...] = (y * (1.0 + sc_ref[...]) + sh_ref[...]).astype(o_ref.dtype)


def _prenorm(x3, mod3, k_shift, k_scale):
    B, T, D = x3.shape
    bt = _tile(T, 512, SUBLANE)
    return pl.pallas_call(
        _prenorm_kernel,
        out_shape=jax.ShapeDtypeStruct((B, T, D), BF16),
        grid=(B, T // bt),
        in_specs=[
            pl.BlockSpec((None, bt, D), lambda b, t: (b, t, 0)),
            pl.BlockSpec((None, 1, D), lambda b, t: (b, 0, k_shift)),
            pl.BlockSpec((None, 1, D), lambda b, t: (b, 0, k_scale)),
        ],
        out_specs=pl.BlockSpec((None, bt, D), lambda b, t: (b, t, 0)),
        compiler_params=_params(("parallel", "parallel")),
        name="prenorm",
    )(x3, mod3, mod3)


def _final_norm_kernel(x_ref, w_ref, o_ref):
    x = x_ref[...]
    o_ref[...] = x * lax.rsqrt(jnp.mean(x * x, axis=-1, keepdims=True) + EPS) * w_ref[...]


def _final_norm(x2, w):
    M, D = x2.shape
    bt = _tile(M, 512, SUBLANE)
    return pl.pallas_call(
        _final_norm_kernel,
        out_shape=jax.ShapeDtypeStruct((M, D), F32),
        grid=(M // bt,),
        in_specs=[pl.BlockSpec((bt, D), lambda i: (i, 0)),
                  pl.BlockSpec((1, D), lambda i: (0, 0))],
        out_specs=pl.BlockSpec((bt, D), lambda i: (i, 0)),
        compiler_params=_params(("parallel",)),
        name="final_norm",
    )(x2, w.reshape(1, D))


def _cast_split(shape, steps):
    rows, cols = shape
    nc = 1
    while nc <= steps:
        nr, rem = divmod(steps, nc)
        if (rem == 0 and rows % nr == 0 and cols % nc == 0
                and (rows // nr) % BF16_SUBLANE == 0 and (cols // nc) % LANE == 0):
            return (rows // nr, cols // nc), nc
        nc *= 2
    return None


def _side_casts(side, grid):
    flat = [s.reshape(-1, s.shape[-1]) for s in side]
    splits = [_cast_split(f.shape, int(np.prod(grid))) for f in flat]
    if any(sp is None for sp in splits):
        return None

    def spec(blk, nc):
        def index(*idx):
            step = functools.reduce(lambda acc, t: acc * t[1] + t[0], zip(idx, grid), 0)
            return step // nc, step % nc
        return pl.BlockSpec(blk, index)

    return (flat, [spec(*sp) for sp in splits],
            [jax.ShapeDtypeStruct(f.shape, BF16) for f in flat])


def _cast_blocks(srcs, dsts):
    for src, dst in zip(srcs, dsts):
        dst[...] = src[...].astype(dst.dtype)


def _mm_cast_kernel(a_ref, w_ref, *refs, n_side):
    o_ref = refs[n_side]
    o_ref[...] = _dot(a_ref[...], w_ref[...]).astype(o_ref.dtype)
    _cast_blocks(refs[:n_side], refs[n_side + 1:])


def _matmul(a, w, layer, out_dtype, side=()):
    M, K = a.shape
    N = w.shape[2]
    bm = _tile(M, 1024, SUBLANE)
    bn = _tile(N, 1024, LANE)
    grid = (M // bm, N // bn)
    casts = _side_casts(side, grid)
    if casts is None:
        return _matmul(a, w, layer, out_dtype)[0], [s.astype(BF16) for s in side]
    flat, side_specs, side_shapes = casts
    outs = pl.pallas_call(
        functools.partial(_mm_cast_kernel, n_side=len(flat)),
        out_shape=[jax.ShapeDtypeStruct((M, N), out_dtype)] + side_shapes,
        grid=grid,
        in_specs=[pl.BlockSpec((bm, K), lambda i, j: (i, 0)),
                  pl.BlockSpec((None, K, bn), lambda i, j: (layer, 0, j))] + side_specs,
        out_specs=[pl.BlockSpec((bm, bn), lambda i, j: (i, j))] + side_specs,
        compiler_params=_params(("parallel", "parallel"), V7X_VMEM_BYTES - 4 * 1024 * 1024),
        name="matmul",
    )(a, w, *flat)
    return outs[0], [o.reshape(s.shape) for o, s in zip(outs[1:], side)]


def _mm_res_kernel(a_ref, w_ref, x_ref, g_ref, o_ref):
    o_ref[...] = x_ref[...] + g_ref[...] * _dot(a_ref[...], w_ref[...])


def _matmul_residual(a, w, layer, x2, mod3, k_gate, seq, *, blk):
    M, K = a.shape
    N = w.shape[2]
    bm = _tile(seq, blk, SUBLANE)
    bn = _tile(N, blk, LANE)
    nb = N // bn
    per_seq = seq // bm
    return pl.pallas_call(
        _mm_res_kernel,
        out_shape=jax.ShapeDtypeStruct((M, N), F32),
        grid=(nb, M // bm),
        in_specs=[
            pl.BlockSpec((bm, K), lambda j, i: (i, 0)),
            pl.BlockSpec((None, K, bn), lambda j, i: (layer, 0, j)),
            pl.BlockSpec((bm, bn), lambda j, i: (i, j)),
            pl.BlockSpec((None, 1, bn), lambda j, i: (i // per_seq, 0, k_gate * nb + j)),
        ],
        out_specs=pl.BlockSpec((bm, bn), lambda j, i: (i, j)),
        compiler_params=_params(("parallel", "parallel")),
        name="matmul_residual",
    )(a, w, x2, mod3)


def _out_norm_kernel(a_ref, w_ref, x_ref, g_ref, sh_ref, sc_ref, xo_ref, ho_ref):
    xn = x_ref[...] + g_ref[...] * _dot(a_ref[...], w_ref[...])
    xo_ref[...] = xn
    y = xn * lax.rsqrt(jnp.mean(xn * xn, axis=-1, keepdims=True) + EPS)
    ho_ref[...] = (y * (1.0 + sc_ref[...]) + sh_ref[...]).astype(ho_ref.dtype)


def _out_proj_norm(a, w, layer, x2, mod3, k_gate, k_shift, k_scale, seq, *, bm):
    M, K = a.shape
    N = w.shape[2]
    per_seq = seq // bm

    def mod(k):
        return pl.BlockSpec((None, 1, N), lambda i: (i // per_seq, 0, k))

    return pl.pallas_call(
        _out_norm_kernel,
        out_shape=(jax.ShapeDtypeStruct((M, N), F32), jax.ShapeDtypeStruct((M, N), BF16)),
        grid=(M // bm,),
        in_specs=[
            pl.BlockSpec((bm, K), lambda i: (i, 0)),
            pl.BlockSpec((None, K, N), lambda i: (layer, 0, 0), pipeline_mode=pl.Buffered(1)),
            pl.BlockSpec((bm, N), lambda i: (i, 0)),
            mod(k_gate), mod(k_shift), mod(k_scale),
        ],
        out_specs=(pl.BlockSpec((bm, N), lambda i: (i, 0)), pl.BlockSpec((bm, N), lambda i: (i, 0))),
        compiler_params=_params(("parallel",), V7X_VMEM_BYTES - 4 * 1024 * 1024),
        name="out_proj_norm",
    )(a, w, x2, mod3, mod3, mod3)


def _causal_conv3(p, halo, w):
    row = lax.broadcasted_iota(jnp.int32, p.shape, 0)
    s1 = pltpu.roll(p, 1, axis=0)
    s2 = pltpu.roll(p, 2, axis=0)
    h7 = halo[SUBLANE - 1:SUBLANE, :]
    h6 = halo[SUBLANE - 2:SUBLANE - 1, :]
    s1 = jnp.where(row == 0, h7, s1)
    s2 = jnp.where(row == 0, h6, jnp.where(row == 1, h7, s2))
    return w[0:1, :] * s2 + w[1:2, :] * s1 + w[2:3, :] * p


def _load_halo(halo_ref, j, first):
    h = halo_ref[j]
    return jnp.where(first, jnp.zeros_like(h), h)


def _sconv_in_kernel(a_ref, wb_ref, wc_ref, wx_ref, cw_ref, o_ref, halo_ref, *, per_seq):
    i = pl.program_id(0)
    j = pl.program_id(1)
    a = a_ref[...]
    p = _dot(a, wc_ref[...]) * _dot(a, wx_ref[...])
    halo = _load_halo(halo_ref, j, i % per_seq == 0)
    halo_ref[j] = p[p.shape[0] - SUBLANE:, :]
    o_ref[...] = (_dot(a, wb_ref[...]) * _causal_conv3(p, halo, cw_ref[...])).astype(o_ref.dtype)


def _sconv_in(a, w_in, layer, conv_w3, seq):
    M, D = a.shape
    bm = _tile(seq, 1024, SUBLANE)
    bn = _tile(D, 512, LANE)
    nb = D // bn
    return pl.pallas_call(
        functools.partial(_sconv_in_kernel, per_seq=seq // bm),
        out_shape=jax.ShapeDtypeStruct((M, D), BF16),
        grid=(M // bm, nb),
        in_specs=[
            pl.BlockSpec((bm, D), lambda i, j: (i, 0)),
            pl.BlockSpec((None, D, bn), lambda i, j: (layer, 0, j)),
            pl.BlockSpec((None, D, bn), lambda i, j: (layer, 0, nb + j)),
            pl.BlockSpec((None, D, bn), lambda i, j: (layer, 0, 2 * nb + j)),
            pl.BlockSpec((3, bn), lambda i, j: (0, j)),
        ],
        out_specs=pl.BlockSpec((bm, bn), lambda i, j: (i, j)),
        scratch_shapes=[pltpu.VMEM((nb, SUBLANE, bn), F32)],
        compiler_params=_params(("arbitrary", "arbitrary")),
        name="sconv_in",
    )(a, w_in, w_in, w_in, conv_w3)


def _ffn_up_kernel(a_ref, wa_ref, wu0_ref, wu1_ref, cw_ref, cb_ref, o_ref, halo_ref,
                   *, per_seq, ragged):
    i = pl.program_id(0)
    j = pl.program_id(1)
    bn = o_ref.shape[1]
    half = bn // 2

    def compute(width):
        a = a_ref[...]
        p = _dot(a, wa_ref[:, :width])
        halo = _load_halo(halo_ref, j, i % per_seq == 0)[:, :width]
        halo_ref[j, :, :width] = p[p.shape[0] - SUBLANE:, :]
        gate = _silu(_causal_conv3(p, halo, cw_ref[:, :width]) + cb_ref[:, :width])
        u = _dot(a, wu0_ref[...])
        if width == bn:
            u = jnp.concatenate([u, _dot(a, wu1_ref[...])], axis=1)
        o_ref[:, :width] = (gate * u).astype(o_ref.dtype)

    if ragged:
        last = pl.num_programs(1) - 1
        pl.when(j < last)(lambda: compute(bn))
        pl.when(j == last)(lambda: compute(half))
    else:
        compute(bn)


def _ffn_up(a, w_up, layer, conv_w3, conv_b, seq):
    M, D = a.shape
    F = w_up.shape[2] // 2
    bm = _tile(seq, 1024, SUBLANE)
    bn = FFN_BN
    half = bn // 2
    assert F % half == 0
    nb = pl.cdiv(F, bn)
    u0 = F // half
    last = 2 * F // half - 1
    return pl.pallas_call(
        functools.partial(_ffn_up_kernel, per_seq=seq // bm, ragged=F % bn != 0),
        out_shape=jax.ShapeDtypeStruct((M, F), BF16),
        grid=(M // bm, nb),
        in_specs=[
            pl.BlockSpec((bm, D), lambda i, j: (i, 0)),
            pl.BlockSpec((None, D, bn), lambda i, j: (layer, 0, j)),
            pl.BlockSpec((None, D, half), lambda i, j: (layer, 0, u0 + 2 * j)),
            pl.BlockSpec((None, D, half), lambda i, j: (layer, 0, jnp.minimum(u0 + 2 * j + 1, last))),
            pl.BlockSpec((3, bn), lambda i, j: (0, j)),
            pl.BlockSpec((1, bn), lambda i, j: (0, j)),
        ],
        out_specs=pl.BlockSpec((bm, bn), lambda i, j: (i, j)),
        scratch_shapes=[pltpu.VMEM((nb, SUBLANE, bn), F32)],
        compiler_params=_params(("arbitrary", "arbitrary")),
        name="ffn_up",
    )(a, w_up, w_up, w_up, conv_w3, conv_b)


N_FINE = 3
LOG_SPLIT = 2


def _hgrn_constants(C, W):
    u = np.arange(C)[None, :]
    t = np.arange(C)[:, None]
    bands = [u <= t]
    masks = []
    for l in range(N_FINE):
        h = 1 << l
        m = (t // (2 * h)) * (2 * h) + h
        upper = t >= m
        bands.append(np.where(upper, (u > m) & (u <= t), (u > t) & (u <= m)))
        masks.append(np.broadcast_to(upper[:SUBLANE], (SUBLANE, W)))
    d = np.concatenate(bands, axis=0).astype(np.float32)
    dmat = np.concatenate([d] * LOG_SPLIT, axis=1)
    rowmask = np.concatenate(masks, axis=0).astype(np.float32)
    s = np.arange(2 * C)[None, :] % C
    x = t ^ s
    lev = np.where(t == s, 0, np.where(t > s, np.floor(np.log2(np.maximum(x, 1))) + 1, -1))
    hh = np.arange(W) // HEAD
    mean = (hh[:, None] == hh[None, :]).astype(np.float32) / HEAD
    return (jnp.asarray(dmat, BF16), jnp.asarray(rowmask, F32),
            jnp.asarray(lev, jnp.int32), jnp.asarray(mean, BF16))


def _hgrn_kernel(q_ref, f_ref, v_ref, g_ref, lbl_ref, gn_ref, dmat_ref, rmask_ref, lev_ref,
                 mean_ref, *refs, n_chunks, slot, group, n_side):
    o_ref, st_ref = refs[n_side], refs[-1]
    _cast_blocks(refs[:n_side], refs[n_side + 1:-1])
    C = CHUNK
    W = 2 * HEAD
    n_lvl = int(np.log2(C))

    @pl.when(pl.program_id(2) == 0)
    def _():
        st_ref[...] = jnp.zeros_like(st_ref)

    rows = [lbl_ref[k:k + 1, :] for k in range(lbl_ref.shape[0])]
    mx = functools.reduce(jnp.maximum, rows)
    ex = [jnp.exp(r - mx) for r in rows]
    lb = functools.reduce(jnp.add, ex[:slot + 1]) / functools.reduce(jnp.add, ex)
    gn = gn_ref[...]
    zc = jnp.zeros((C, HEAD), BF16)
    zs = jnp.zeros((HEAD, HEAD), BF16)

    def by_head(x):
        return jnp.concatenate(
            [jnp.concatenate([x[:, :HEAD], zc], axis=1),
             jnp.concatenate([zc, x[:, HEAD:]], axis=1)], axis=0)

    G = group
    R = G * C

    def per_chunk(x):
        return [x[c * C:(c + 1) * C] for c in range(G)]

    def body(gi, carry):
        rs = pl.ds(pl.multiple_of(gi * R, R), R)
        q_all = q_ref[rs, :]
        f_all = lb + (1.0 - lb) * jax.nn.sigmoid(f_ref[rs, :])
        k_all = 1.0 - f_all
        lg = jnp.log(f_all) * LOG2E
        pieces, rem = [], lg
        for _ in range(LOG_SPLIT - 1):
            pieces.append(rem.astype(BF16))
            rem = rem - pieces[-1].astype(F32)
        pieces.append(rem.astype(BF16))
        q, k = per_chunk(q_all), per_chunk(k_all)
        vb = per_chunk(v_ref[rs, :].astype(BF16))
        dmat = dmat_ref[...]
        args = [_dot(dmat, jnp.concatenate(list(xs), axis=0))
                for xs in zip(*[per_chunk(x) for x in pieces])]
        b = [x[0:C] for x in args]
        b_last = [x[C - 1:C, :] for x in b]
        e_fine = [jnp.exp2(x[C:]) for x in args]
        lev = lev_ref[...]

        a = [jnp.where(lev == 0, _dot_nt(q[c].astype(BF16), by_head(k[c].astype(BF16))), 0.0)
             for c in range(G)]
        for l in range(N_FINE):
            up = jnp.tile(rmask_ref[l * SUBLANE:(l + 1) * SUBLANE, :], (C // SUBLANE, 1)) != 0.0
            for c in range(G):
                z = (jnp.where(up, q[c], k[c]) * e_fine[c][l * C:(l + 1) * C]).astype(BF16)
                a[c] = jnp.where(lev == l + 1, _dot_nt(z, by_head(z)), a[c])
        for l in range(N_FINE, n_lvl):
            h = 1 << l
            zh = jnp.zeros((h, W), BF16)
            for c in range(G):
                qe, ke = [], []
                for base in range(0, C, 2 * h):
                    m = base + h
                    bm = b[c][m:m + 1, :]
                    qe += [zh, (q[c][m:m + h] * jnp.exp2(b[c][m:m + h] - bm)).astype(BF16)]
                    ke += [(k[c][base:m] * jnp.exp2(bm - b[c][base:m])).astype(BF16), zh]
                p = _dot_nt(jnp.concatenate(qe, axis=0), by_head(jnp.concatenate(ke, axis=0)))
                a[c] = jnp.where(lev == l + 1, p, a[c])

        qh = [(q[c] * jnp.exp2(b[c])).astype(BF16) for c in range(G)]
        kh = [(k[c] * jnp.exp2(b_last[c] - b[c])).astype(BF16) for c in range(G)]
        upd0 = [_dot_tn(vb[c][:, :HEAD], kh[c][:, :HEAD]) for c in range(G)]
        upd1 = [_dot_tn(vb[c][:, HEAD:], kh[c][:, HEAD:]) for c in range(G)]
        o_intra = [_dot(a[c].astype(BF16), by_head(vb[c])) for c in range(G)]
        d_s = [jnp.exp2(x) for x in b_last]
        s0 = st_ref[0]
        s1 = st_ref[1]
        o = []
        for c in range(G):
            s_bd = jnp.concatenate(
                [jnp.concatenate([s0.astype(BF16), zs], axis=1),
                 jnp.concatenate([zs, s1.astype(BF16)], axis=1)], axis=0)
            o.append(_dot_nt(qh[c], s_bd) + o_intra[c])
            s0 = s0 * d_s[c][:, :HEAD] + upd0[c]
            s1 = s1 * d_s[c][:, HEAD:] + upd1[c]
        st_ref[0] = s0
        st_ref[1] = s1

        o_all = jnp.concatenate(o, axis=0)
        ms = _dot((o_all * o_all).astype(BF16), mean_ref[...])
        g = g_ref[rs, :]
        o_ref[rs, :] = (o_all * lax.rsqrt(ms + EPS) * gn * _silu(g)).astype(o_ref.dtype)
        return carry

    lax.fori_loop(0, n_chunks // G, body, 0)


def _hgrn_recurrence(proj, lb_logits, g_norm, batch, seq, slot, side=()):
    M, D4 = proj.shape
    D = D4 // 4
    W = 2 * HEAD
    npair = D // W
    tb = _tile(seq, 2048, CHUNK)
    nt = seq // tb
    group = _tile(tb // CHUNK, 8, 1)
    consts = _hgrn_constants(CHUNK, W)

    def col(part):
        return pl.BlockSpec((tb, W), lambda b, p, t: (b * nt + t, part * npair + p))

    def whole(arr):
        return pl.BlockSpec(arr.shape, lambda b, p, t: (0, 0))

    grid = (batch, npair, nt)
    casts = _side_casts(side, grid)
    if casts is None:
        return (_hgrn_recurrence(proj, lb_logits, g_norm, batch, seq, slot)[0],
                [s.astype(BF16) for s in side])
    flat, side_specs, side_shapes = casts
    outs = pl.pallas_call(
        functools.partial(_hgrn_kernel, n_chunks=tb // CHUNK, slot=slot, group=group,
                          n_side=len(flat)),
        out_shape=[jax.ShapeDtypeStruct((M, D), BF16)] + side_shapes,
        grid=grid,
        in_specs=[col(0), col(1), col(2), col(3),
                  pl.BlockSpec((lb_logits.shape[0], W), lambda b, p, t: (0, p)),
                  pl.BlockSpec((1, W), lambda b, p, t: (0, p))]
        + [whole(c) for c in consts] + side_specs,
        out_specs=[pl.BlockSpec((tb, W), lambda b, p, t: (b * nt + t, p))] + side_specs,
        scratch_shapes=[pltpu.VMEM((2, HEAD, HEAD), F32)],
        compiler_params=_params(("parallel", "parallel", "arbitrary")),
        name="hgrn_recurrence",
    )(proj, proj, proj, proj, lb_logits, g_norm.reshape(1, D), *consts, *flat)
    return outs[0], [o.reshape(s.shape) for o, s in zip(outs[1:], side)]


def kernel(x, c, w_cond, b_cond, ada_w, ada_b, lb_logits, hgrn_w_in, hgrn_norm, hgrn_w_out,
           sconv_w_in, sconv_conv_w, sconv_w_out, ffn_w_up, ffn_conv_w, ffn_conv_b,
           ffn_w_down, final_norm):
    B, T, D = x.shape
    depth = ada_w.shape[0]
    F = ffn_conv_b.shape[1]
    M = B * T
    n_mixers = 2

    c_emb = _cond_matmul(c, w_cond[None], b_cond[None], act=True)[0]
    mod = _cond_matmul(c_emb, ada_w, ada_b, act=False)

    w_hin = hgrn_w_in.astype(BF16)

    x2 = x.reshape(M, D)
    for i in range(depth):
        mod3 = mod[i].reshape(B, 1, N_MOD * D)
        j = i // n_mixers
        hm = _prenorm(x2.reshape(B, T, D), mod3, 0, 1).reshape(M, D)
        if i % n_mixers == 0:
            first = i == 0
            proj, copies = _matmul(hm, w_hin, j, F32, side=(hgrn_w_out, ffn_w_up) if first else ())
            if first:
                w_hout, w_up = copies
            y, copies = _hgrn_recurrence(proj, lb_logits, hgrn_norm[j], B, T, slot=i,
                                         side=(sconv_w_in, sconv_w_out, ffn_w_down) if first else ())
            if first:
                w_sin, w_sout, w_down = copies
            x2, hf = _out_proj_norm(y, w_hout, j, x2, mod3, 2, 3, 4, T, bm=OUT_BM)
        else:
            y = _sconv_in(hm, w_sin, j, sconv_conv_w[j].T, T)
            x2, hf = _out_proj_norm(y, w_sout, j, x2, mod3, 2, 3, 4, T, bm=OUT_BM)

        h = _ffn_up(hf, w_up, i, ffn_conv_w[i].T, ffn_conv_b[i].reshape(1, F), T)
        x2 = _matmul_residual(h, w_down, i, x2, mod3, 5, T, blk=512)
    return _final_norm(x2, final_norm).reshape(B, T, D)
```

```python
import functools

import numpy as np
import jax
import jax.numpy as jnp
from jax import lax
from jax.experimental import pallas as pl
from jax.experimental.pallas import tpu as pltpu

EPS = 1e-6
CHUNK = 64
HEAD = 128
N_MOD = 6
LOG2E = 1.4426950408889634
F32 = jnp.float32
BF16 = jnp.bfloat16

V7X_VMEM_BYTES = 64 * 1024 * 1024
VMEM_LIMIT = V7X_VMEM_BYTES - 8 * 1024 * 1024
LANE = 128
SUBLANE = 8
BF16_SUBLANE = 16
FFN_BN = 512
OUT_BM = 256


def _tile(n, pref, align):
    t = min(pref, n)
    t -= t % align
    while t >= align:
        if n % t == 0:
            return t
        t -= align
    return n


def _params(sem, vmem=VMEM_LIMIT):
    return pltpu.CompilerParams(dimension_semantics=sem, vmem_limit_bytes=vmem)


def _dot(a, b):
    return jnp.dot(a, b, preferred_element_type=F32)


def _dot_nt(a, b):
    return lax.dot_general(a, b, (((1,), (1,)), ((), ())), preferred_element_type=F32)


def _dot_tn(a, b):
    return lax.dot_general(a, b, (((0,), (0,)), ((), ())), preferred_element_type=F32)


def _silu(x):
    return x * jax.nn.sigmoid(x)


def _cond_kernel(c_ref, w_ref, b_ref, o_ref, *, act):
    acc = _dot(c_ref[...].astype(BF16), w_ref[...].astype(BF16)) + b_ref[...]
    o_ref[...] = _silu(acc) if act else acc


def _cond_matmul(c, w, b, *, act):
    L, K, N = w.shape
    B = c.shape[0]
    bn = _tile(N, 512, LANE)
    return pl.pallas_call(
        functools.partial(_cond_kernel, act=act),
        out_shape=jax.ShapeDtypeStruct((L, B, N), F32),
        grid=(L, N // bn),
        in_specs=[
            pl.BlockSpec((B, K), lambda l, j: (0, 0)),
            pl.BlockSpec((None, K, bn), lambda l, j: (l, 0, j)),
            pl.BlockSpec((None, 1, bn), lambda l, j: (l, 0, j)),
        ],
        out_specs=pl.BlockSpec((None, B, bn), lambda l, j: (l, 0, j)),
        compiler_params=_params(("parallel", "parallel")),
        name="cond_matmul",
    )(c, w, b.reshape(L, 1, N))


def _prenorm_kernel(x_ref, sh_ref, sc_ref, o_ref):
    x = x_ref[...]
    y = x * lax.rsqrt(jnp.mean(x * x, axis=-1, keepdims=True) + EPS)
    o_ref[...] = (y * (1.0 + sc_ref[...]) + sh_ref[...]).astype(o_ref.dtype)


def _prenorm(x3, mod3, k_shift, k_scale):
    B, T, D = x3.shape
    bt = _tile(T, 512, SUBLANE)
    return pl.pallas_call(
        _prenorm_kernel,
        out_shape=jax.ShapeDtypeStruct((B, T, D), BF16),
        grid=(B, T // bt),
        in_specs=[
            pl.BlockSpec((None, bt, D), lambda b, t: (b, t, 0)),
            pl.BlockSpec((None, 1, D), lambda b, t: (b, 0, k_shift)),
            pl.BlockSpec((None, 1, D), lambda b, t: (b, 0, k_scale)),
        ],
        out_specs=pl.BlockSpec((None, bt, D), lambda b, t: (b, t, 0)),
        compiler_params=_params(("parallel", "parallel")),
        name="prenorm",
    )(x3, mod3, mod3)


def _final_norm_kernel(x_ref, w_ref, o_ref):
    x = x_ref[...]
    o_ref[...] = x * lax.rsqrt(jnp.mean(x * x, axis=-1, keepdims=True) + EPS) * w_ref[...]


def _final_norm(x2, w):
    M, D = x2.shape
    bt = _tile(M, 512, SUBLANE)
    return pl.pallas_call(
        _final_norm_kernel,
        out_shape=jax.ShapeDtypeStruct((M, D), F32),
        grid=(M // bt,),
        in_specs=[pl.BlockSpec((bt, D), lambda i: (i, 0)),
                  pl.BlockSpec((1, D), lambda i: (0, 0))],
        out_specs=pl.BlockSpec((bt, D), lambda i: (i, 0)),
        compiler_params=_params(("parallel",)),
        name="final_norm",
    )(x2, w.reshape(1, D))


def _cast_split(shape, steps):
    rows, cols = shape
    nc = 1
    while nc <= steps:
        nr, rem = divmod(steps, nc)
        if (rem == 0 and rows % nr == 0 and cols % nc == 0
                and (rows // nr) % BF16_SUBLANE == 0 and (cols // nc) % LANE == 0):
            return (rows // nr, cols // nc), nc
        nc *= 2
    return None


def _side_casts(side, grid):
    flat = [s.reshape(-1, s.shape[-1]) for s in side]
    splits = [_cast_split(f.shape, int(np.prod(grid))) for f in flat]
    if any(sp is None for sp in splits):
        return None

    def spec(blk, nc):
        def index(*idx):
            step = functools.reduce(lambda acc, t: acc * t[1] + t[0], zip(idx, grid), 0)
            return step // nc, step % nc
        return pl.BlockSpec(blk, index)

    return (flat, [spec(*sp) for sp in splits],
            [jax.ShapeDtypeStruct(f.shape, BF16) for f in flat])


def _cast_blocks(srcs, dsts):
    for src, dst in zip(srcs, dsts):
        dst[...] = src[...].astype(dst.dtype)


def _mm_cast_kernel(a_ref, w_ref, *refs, n_side):
    o_ref = refs[n_side]
    o_ref[...] = _dot(a_ref[...], w_ref[...]).astype(o_ref.dtype)
    _cast_blocks(refs[:n_side], refs[n_side + 1:])


def _matmul(a, w, layer, out_dtype, side=()):
    M, K = a.shape
    N = w.shape[2]
    bm = _tile(M, 1024, SUBLANE)
    bn = _tile(N, 1024, LANE)
    grid = (N // bn, M // bm)
    casts = _side_casts(side, grid)
    if casts is None:
        return _matmul(a, w, layer, out_dtype)[0], [s.astype(BF16) for s in side]
    flat, side_specs, side_shapes = casts
    outs = pl.pallas_call(
        functools.partial(_mm_cast_kernel, n_side=len(flat)),
        out_shape=[jax.ShapeDtypeStruct((M, N), out_dtype)] + side_shapes,
        grid=grid,
        in_specs=[pl.BlockSpec((bm, K), lambda j, i: (i, 0)),
                  pl.BlockSpec((None, K, bn), lambda j, i: (layer, 0, j))] + side_specs,
        out_specs=[pl.BlockSpec((bm, bn), lambda j, i: (i, j))] + side_specs,
        compiler_params=_params(("parallel", "parallel"), V7X_VMEM_BYTES - 4 * 1024 * 1024),
        name="matmul",
    )(a, w, *flat)
    return outs[0], [o.reshape(s.shape) for o, s in zip(outs[1:], side)]


def _mm_res_kernel(a_ref, w_ref, x_ref, g_ref, o_ref):
    o_ref[...] = x_ref[...] + g_ref[...] * _dot(a_ref[...], w_ref[...])


def _matmul_residual(a, w, layer, x2, mod3, k_gate, seq, *, blk):
    M, K = a.shape
    N = w.shape[2]
    bm = _tile(seq, blk, SUBLANE)
    bn = _tile(N, blk, LANE)
    nb = N // bn
    per_seq = seq // bm
    return pl.pallas_call(
        _mm_res_kernel,
        out_shape=jax.ShapeDtypeStruct((M, N), F32),
        grid=(nb, M // bm),
        in_specs=[
            pl.BlockSpec((bm, K), lambda j, i: (i, 0)),
            pl.BlockSpec((None, K, bn), lambda j, i: (layer, 0, j)),
            pl.BlockSpec((bm, bn), lambda j, i: (i, j)),
            pl.BlockSpec((None, 1, bn), lambda j, i: (i // per_seq, 0, k_gate * nb + j)),
        ],
        out_specs=pl.BlockSpec((bm, bn), lambda j, i: (i, j)),
        compiler_params=_params(("parallel", "parallel")),
        name="matmul_residual",
    )(a, w, x2, mod3)


def _out_norm_kernel(a_ref, w_ref, x_ref, g_ref, sh_ref, sc_ref, xo_ref, ho_ref):
    xn = x_ref[...] + g_ref[...] * _dot(a_ref[...], w_ref[...])
    xo_ref[...] = xn
    y = xn * lax.rsqrt(jnp.mean(xn * xn, axis=-1, keepdims=True) + EPS)
    ho_ref[...] = (y * (1.0 + sc_ref[...]) + sh_ref[...]).astype(ho_ref.dtype)


def _out_proj_norm(a, w, layer, x2, mod3, k_gate, k_shift, k_scale, seq, *, bm):
    M, K = a.shape
    N = w.shape[2]
    per_seq = seq // bm

    def mod(k):
        return pl.BlockSpec((None, 1, N), lambda i: (i // per_seq, 0, k))

    return pl.pallas_call(
        _out_norm_kernel,
        out_shape=(jax.ShapeDtypeStruct((M, N), F32), jax.ShapeDtypeStruct((M, N), BF16)),
        grid=(M // bm,),
        in_specs=[
            pl.BlockSpec((bm, K), lambda i: (i, 0)),
            pl.BlockSpec((None, K, N), lambda i: (layer, 0, 0), pipeline_mode=pl.Buffered(1)),
            pl.BlockSpec((bm, N), lambda i: (i, 0)),
            mod(k_gate), mod(k_shift), mod(k_scale),
        ],
        out_specs=(pl.BlockSpec((bm, N), lambda i: (i, 0)), pl.BlockSpec((bm, N), lambda i: (i, 0))),
        compiler_params=_params(("parallel",), V7X_VMEM_BYTES - 4 * 1024 * 1024),
        name="out_proj_norm",
    )(a, w, x2, mod3, mod3, mod3)


def _causal_conv3(p, halo, w):
    row = lax.broadcasted_iota(jnp.int32, p.shape, 0)
    s1 = pltpu.roll(p, 1, axis=0)
    s2 = pltpu.roll(p, 2, axis=0)
    h7 = halo[SUBLANE - 1:SUBLANE, :]
    h6 = halo[SUBLANE - 2:SUBLANE - 1, :]
    s1 = jnp.where(row == 0, h7, s1)
    s2 = jnp.where(row == 0, h6, jnp.where(row == 1, h7, s2))
    return w[0:1, :] * s2 + w[1:2, :] * s1 + w[2:3, :] * p


def _sconv_in_kernel(a_ref, wb_ref, wc_ref, wx_ref, cw_ref, o_ref, halo_ref, *, per_seq):
    i = pl.program_id(1)
    a = a_ref[...]
    p = _dot(a, wc_ref[...]) * _dot(a, wx_ref[...])
    halo = halo_ref[...]
    halo = jnp.where(i % per_seq == 0, jnp.zeros_like(halo), halo)
    halo_ref[...] = p[p.shape[0] - SUBLANE:, :]
    o_ref[...] = (_dot(a, wb_ref[...]) * _causal_conv3(p, halo, cw_ref[...])).astype(o_ref.dtype)


def _sconv_in(a, w_in, layer, conv_w3, seq):
    M, D = a.shape
    bm = _tile(seq, 1024, SUBLANE)
    bn = _tile(D, 512, LANE)
    nb = D // bn
    return pl.pallas_call(
        functools.partial(_sconv_in_kernel, per_seq=seq // bm),
        out_shape=jax.ShapeDtypeStruct((M, D), BF16),
        grid=(nb, M // bm),
        in_specs=[
            pl.BlockSpec((bm, D), lambda j, i: (i, 0)),
            pl.BlockSpec((None, D, bn), lambda j, i: (layer, 0, j)),
            pl.BlockSpec((None, D, bn), lambda j, i: (layer, 0, nb + j)),
            pl.BlockSpec((None, D, bn), lambda j, i: (layer, 0, 2 * nb + j)),
            pl.BlockSpec((3, bn), lambda j, i: (0, j)),
        ],
        out_specs=pl.BlockSpec((bm, bn), lambda j, i: (i, j)),
        scratch_shapes=[pltpu.VMEM((SUBLANE, bn), F32)],
        compiler_params=_params(("arbitrary", "arbitrary")),
        name="sconv_in",
    )(a, w_in, w_in, w_in, conv_w3)


def _ffn_up_kernel(a_ref, wa_ref, wu0_ref, wu1_ref, cw_ref, cb_ref, o_ref, halo_ref,
                   *, per_seq, ragged):
    j = pl.program_id(0)
    i = pl.program_id(1)
    bn = o_ref.shape[1]
    half = bn // 2

    def compute(width):
        a = a_ref[...]
        p = _dot(a, wa_ref[:, :width])
        halo = halo_ref[:, :width]
        halo = jnp.where(i % per_seq == 0, jnp.zeros_like(halo), halo)
        halo_ref[:, :width] = p[p.shape[0] - SUBLANE:, :]
        gate = _silu(_causal_conv3(p, halo, cw_ref[:, :width]) + cb_ref[:, :width])
        u = _dot(a, wu0_ref[...])
        if width == bn:
            u = jnp.concatenate([u, _dot(a, wu1_ref[...])], axis=1)
        o_ref[:, :width] = (gate * u).astype(o_ref.dtype)

    if ragged:
        last = pl.num_programs(0) - 1
        pl.when(j < last)(lambda: compute(bn))
        pl.when(j == last)(lambda: compute(half))
    else:
        compute(bn)


def _ffn_up(a, w_up, layer, conv_w3, conv_b, seq):
    M, D = a.shape
    F = w_up.shape[2] // 2
    bm = _tile(seq, 1024, SUBLANE)
    bn = FFN_BN
    half = bn // 2
    assert F % half == 0
    nb = pl.cdiv(F, bn)
    u0 = F // half
    last = 2 * F // half - 1
    return pl.pallas_call(
        functools.partial(_ffn_up_kernel, per_seq=seq // bm, ragged=F % bn != 0),
        out_shape=jax.ShapeDtypeStruct((M, F), BF16),
        grid=(nb, M // bm),
        in_specs=[
            pl.BlockSpec((bm, D), lambda j, i: (i, 0)),
            pl.BlockSpec((None, D, bn), lambda j, i: (layer, 0, j)),
            pl.BlockSpec((None, D, half), lambda j, i: (layer, 0, u0 + 2 * j)),
            pl.BlockSpec((None, D, half), lambda j, i: (layer, 0, jnp.minimum(u0 + 2 * j + 1, last))),
            pl.BlockSpec((3, bn), lambda j, i: (0, j)),
            pl.BlockSpec((1, bn), lambda j, i: (0, j)),
        ],
        out_specs=pl.BlockSpec((bm, bn), lambda j, i: (i, j)),
        scratch_shapes=[pltpu.VMEM((SUBLANE, bn), F32)],
        compiler_params=_params(("arbitrary", "arbitrary")),
        name="ffn_up",
    )(a, w_up, w_up, w_up, conv_w3, conv_b)


N_FINE = 3
LOG_SPLIT = 2


def _hgrn_constants(C, W):
    u = np.arange(C)[None, :]
    t = np.arange(C)[:, None]
    bands = [u <= t]
    masks = []
    for l in range(1, N_FINE):
        h = 1 << l
        m = (t // (2 * h)) * (2 * h) + h
        upper = t >= m
        bands.append(np.where(upper, (u > m) & (u <= t), (u > t) & (u <= m)))
        masks.append(np.broadcast_to(upper[:SUBLANE], (SUBLANE, W)))
    d = np.concatenate(bands, axis=0).astype(np.float32)
    dmat = np.concatenate([d] * LOG_SPLIT, axis=1)
    rowmask = np.concatenate(masks, axis=0).astype(np.float32)
    s = np.arange(2 * C)[None, :] % C
    x = t ^ s
    lev = np.where(t > s, np.floor(np.log2(np.maximum(x, 1))), -1)
    return jnp.asarray(dmat, BF16), jnp.asarray(rowmask, F32), jnp.asarray(lev, jnp.int32)


def _hgrn_kernel(q_ref, f_ref, v_ref, g_ref, lbl_ref, gn_ref, dmat_ref, rmask_ref, lev_ref,
                 *refs, n_chunks, slot, group, n_side):
    o_ref, st_ref = refs[n_side], refs[-1]
    _cast_blocks(refs[:n_side], refs[n_side + 1:-1])
    C = CHUNK
    W = 2 * HEAD
    n_lvl = int(np.log2(C))

    @pl.when(pl.program_id(2) == 0)
    def _():
        st_ref[...] = jnp.zeros_like(st_ref)

    rows = [lbl_ref[k:k + 1, :] for k in range(lbl_ref.shape[0])]
    mx = functools.reduce(jnp.maximum, rows)
    ex = [jnp.exp(r - mx) for r in rows]
    lb = functools.reduce(jnp.add, ex[:slot + 1]) / functools.reduce(jnp.add, ex)
    gn = gn_ref[...]
    zc = jnp.zeros((C, HEAD), BF16)
    zs = jnp.zeros((HEAD, HEAD), BF16)

    def by_head(x):
        return jnp.concatenate(
            [jnp.concatenate([x[:, :HEAD], zc], axis=1),
             jnp.concatenate([zc, x[:, HEAD:]], axis=1)], axis=0)

    G = group
    R = G * C

    def per_chunk(x):
        return [x[c * C:(c + 1) * C] for c in range(G)]

    def body(gi, carry):
        rs = pl.ds(pl.multiple_of(gi * R, R), R)
        q_all = q_ref[rs, :]
        f_all = lb + (1.0 - lb) * jax.nn.sigmoid(f_ref[rs, :])
        k_all = 1.0 - f_all
        lg = jnp.log(f_all) * LOG2E
        pieces, rem = [], lg
        for _ in range(LOG_SPLIT - 1):
            pieces.append(rem.astype(BF16))
            rem = rem - pieces[-1].astype(F32)
        pieces.append(rem.astype(BF16))
        q, k, f = per_chunk(q_all), per_chunk(k_all), per_chunk(f_all)
        v_all = v_ref[rs, :]
        v = per_chunk(v_all)
        vb = per_chunk(v_all.astype(BF16))
        dmat = dmat_ref[...]
        args = [_dot(dmat, jnp.concatenate(list(xs), axis=0))
                for xs in zip(*[per_chunk(x) for x in pieces])]
        b = [x[0:C] for x in args]
        b_last = [x[C - 1:C, :] for x in b]
        e_fine = [jnp.exp2(x[C:]) for x in args]
        lev = lev_ref[...]

        a = [0.0] * G
        for l in range(1, N_FINE):
            up = jnp.tile(rmask_ref[(l - 1) * SUBLANE:l * SUBLANE, :], (C // SUBLANE, 1)) != 0.0
            for c in range(G):
                z = (jnp.where(up, q[c], k[c]) * e_fine[c][(l - 1) * C:l * C]).astype(BF16)
                a[c] = jnp.where(lev == l, _dot_nt(z, by_head(z)), a[c])
        for l in range(N_FINE, n_lvl):
            h = 1 << l
            zh = jnp.zeros((h, W), BF16)
            for c in range(G):
                qe, ke = [], []
                for base in range(0, C, 2 * h):
                    m = base + h
                    bm = b[c][m:m + 1, :]
                    qe += [zh, (q[c][m:m + h] * jnp.exp2(b[c][m:m + h] - bm)).astype(BF16)]
                    ke += [(k[c][base:m] * jnp.exp2(bm - b[c][base:m])).astype(BF16), zh]
                p = _dot_nt(jnp.concatenate(qe, axis=0), by_head(jnp.concatenate(ke, axis=0)))
                a[c] = jnp.where(lev == l, p, a[c])

        def per_head(x, y):
            return jnp.concatenate(
                [jnp.sum(x[:, :HEAD], axis=-1, keepdims=True) * y[:, :HEAD],
                 jnp.sum(x[:, HEAD:], axis=-1, keepdims=True) * y[:, HEAD:]], axis=1)

        qh = [(q[c] * jnp.exp2(b[c])).astype(BF16) for c in range(G)]
        kh = [(k[c] * jnp.exp2(b_last[c] - b[c])).astype(BF16) for c in range(G)]
        upd0 = [_dot_tn(vb[c][:, :HEAD], kh[c][:, :HEAD]) for c in range(G)]
        upd1 = [_dot_tn(vb[c][:, HEAD:], kh[c][:, HEAD:]) for c in range(G)]
        o_intra = [_dot(a[c].astype(BF16), by_head(vb[c])) for c in range(G)]
        d_s = [jnp.exp2(x) for x in b_last]
        odd = (lax.broadcasted_iota(jnp.int32, (C, W), 0) & 1) == 1
        s0 = st_ref[0]
        s1 = st_ref[1]
        o = []
        for c in range(G):
            s_bd = jnp.concatenate(
                [jnp.concatenate([s0.astype(BF16), zs], axis=1),
                 jnp.concatenate([zs, s1.astype(BF16)], axis=1)], axis=0)
            near = per_head(q[c] * f[c] * pltpu.roll(k[c], 1, axis=0), pltpu.roll(v[c], 1, axis=0))
            o.append(_dot_nt(qh[c], s_bd) + o_intra[c]
                     + per_head(q[c] * k[c], v[c]) + jnp.where(odd, near, 0.0))
            s0 = s0 * d_s[c][:, :HEAD] + upd0[c]
            s1 = s1 * d_s[c][:, HEAD:] + upd1[c]
        st_ref[0] = s0
        st_ref[1] = s1

        o_all = jnp.concatenate(o, axis=0)
        ms = per_head(o_all * o_all, jnp.full_like(o_all, 1.0 / HEAD))
        g = g_ref[rs, :]
        o_ref[rs, :] = (o_all * lax.rsqrt(ms + EPS) * gn * _silu(g)).astype(o_ref.dtype)
        return carry

    lax.fori_loop(0, n_chunks // G, body, 0)


def _hgrn_recurrence(proj, lb_logits, g_norm, batch, seq, slot, side=()):
    M, D4 = proj.shape
    D = D4 // 4
    W = 2 * HEAD
    npair = D // W
    tb = _tile(seq, 2048, CHUNK)
    nt = seq // tb
    group = _tile(tb // CHUNK, 8, 1)
    consts = _hgrn_constants(CHUNK, W)

    def col(part):
        return pl.BlockSpec((tb, W), lambda b, p, t: (b * nt + t, part * npair + p))

    def whole(arr):
        return pl.BlockSpec(arr.shape, lambda b, p, t: (0, 0))

    grid = (batch, npair, nt)
    casts = _side_casts(side, grid)
    if casts is None:
        return (_hgrn_recurrence(proj, lb_logits, g_norm, batch, seq, slot)[0],
                [s.astype(BF16) for s in side])
    flat, side_specs, side_shapes = casts
    outs = pl.pallas_call(
        functools.partial(_hgrn_kernel, n_chunks=tb // CHUNK, slot=slot, group=group,
                          n_side=len(flat)),
        out_shape=[jax.ShapeDtypeStruct((M, D), BF16)] + side_shapes,
        grid=grid,
        in_specs=[col(0), col(1), col(2), col(3),
                  pl.BlockSpec((lb_logits.shape[0], W), lambda b, p, t: (0, p)),
                  pl.BlockSpec((1, W), lambda b, p, t: (0, p))]
        + [whole(c) for c in consts] + side_specs,
        out_specs=[pl.BlockSpec((tb, W), lambda b, p, t: (b * nt + t, p))] + side_specs,
        scratch_shapes=[pltpu.VMEM((2, HEAD, HEAD), F32)],
        compiler_params=_params(("parallel", "parallel", "arbitrary")),
        name="hgrn_recurrence",
    )(proj, proj, proj, proj, lb_logits, g_norm.reshape(1, D), *consts, *flat)
    return outs[0], [o.reshape(s.shape) for o, s in zip(outs[1:], side)]


def kernel(x, c, w_cond, b_cond, ada_w, ada_b, lb_logits, hgrn_w_in, hgrn_norm, hgrn_w_out,
           sconv_w_in, sconv_conv_w, sconv_w_out, ffn_w_up, ffn_conv_w, ffn_conv_b,
           ffn_w_down, final_norm):
    B, T, D = x.shape
    depth = ada_w.shape[0]
    F = ffn_conv_b.shape[1]
    M = B * T
    n_mixers = 2

    c_emb = _cond_matmul(c, w_cond[None], b_cond[None], act=True)[0]
    mod = _cond_matmul(c_emb, ada_w, ada_b, act=False)

    w_hin = hgrn_w_in.astype(BF16)

    x2 = x.reshape(M, D)
    for i in range(depth):
        mod3 = mod[i].reshape(B, 1, N_MOD * D)
        j = i // n_mixers
        hm = _prenorm(x2.reshape(B, T, D), mod3, 0, 1).reshape(M, D)
        if i % n_mixers == 0:
            first = i == 0
            proj, copies = _matmul(hm, w_hin, j, F32, side=(hgrn_w_out, ffn_w_up) if first else ())
            if first:
                w_hout, w_up = copies
            y, copies = _hgrn_recurrence(proj, lb_logits, hgrn_norm[j], B, T, slot=i,
                                         side=(sconv_w_in, sconv_w_out, ffn_w_down) if first else ())
            if first:
                w_sin, w_sout, w_down = copies
            x2, hf = _out_proj_norm(y, w_hout, j, x2, mod3, 2, 3, 4, T, bm=OUT_BM)
        else:
            y = _sconv_in(hm, w_sin, j, sconv_conv_w[j].T, T)
            x2, hf = _out_proj_norm(y, w_sout, j, x2, mod3, 2, 3, 4, T, bm=OUT_BM)

        h = _ffn_up(hf, w_up, i, ffn_conv_w[i].T, ffn_conv_b[i].reshape(1, F), T)
        x2 = _matmul_residual(h, w_down, i, x2, mod3, 5, T, blk=512)
    return _final_norm(x2, final_norm).reshape(B, T, D)
```

```python
import functools

import numpy as np
import jax
import jax.numpy as jnp
from jax import lax
from jax.experimental import pallas as pl
from jax.experimental.pallas import tpu as pltpu

EPS = 1e-6
CHUNK = 64
HEAD = 128
N_MOD = 6
LOG2E = 1.4426950408889634
F32 = jnp.float32
BF16 = jnp.bfloat16

V7X_VMEM_BYTES = 64 * 1024 * 1024
VMEM_LIMIT = V7X_VMEM_BYTES - 8 * 1024 * 1024
LANE = 128
SUBLANE = 8
BF16_SUBLANE = 16
FFN_BN = 512
OUT_BM = 256


def _tile(n, pref, align):
    t = min(pref, n)
    t -= t % align
    while t >= align:
        if n % t == 0:
            return t
        t -= align
    return n


def _params(sem, vmem=VMEM_LIMIT):
    return pltpu.CompilerParams(dimension_semantics=sem, vmem_limit_bytes=vmem)


def _dot(a, b):
    return jnp.dot(a, b, preferred_element_type=F32)


def _dot_nt(a, b):
    return lax.dot_general(a, b, (((1,), (1,)), ((), ())), preferred_element_type=F32)


def _dot_tn(a, b):
    return lax.dot_general(a, b, (((0,), (0,)), ((), ())), preferred_element_type=F32)


def _silu(x):
    return x * jax.nn.sigmoid(x)


def _cond_kernel(c_ref, w_ref, b_ref, o_ref, *, act):
    acc = _dot(c_ref[...].astype(BF16), w_ref[...].astype(BF16)) + b_ref[...]
    o_ref[...] = _silu(acc) if act else acc


def _cond_matmul(c, w, b, *, act):
    L, K, N = w.shape
    B = c.shape[0]
    bn = _tile(N, 512, LANE)
    return pl.pallas_call(
        functools.partial(_cond_kernel, act=act),
        out_shape=jax.ShapeDtypeStruct((L, B, N), F32),
        grid=(L, N // bn),
        in_specs=[
            pl.BlockSpec((B, K), lambda l, j: (0, 0)),
            pl.BlockSpec((None, K, bn), lambda l, j: (l, 0, j)),
            pl.BlockSpec((None, 1, bn), lambda l, j: (l, 0, j)),
        ],
        out_specs=pl.BlockSpec((None, B, bn), lambda l, j: (l, 0, j)),
        compiler_params=_params(("parallel", "parallel")),
        name="cond_matmul",
    )(c, w, b.reshape(L, 1, N))


def _prenorm_kernel(x_ref, sh_ref, sc_ref, o_ref):
    x = x_ref[...]
    y = x * lax.rsqrt(jnp.mean(x * x, axis=-1, keepdims=True) + EPS)
    o_ref[...] = (y * (1.0 + sc_ref[...]) + sh_ref[...]).astype(o_ref.dtype)


def _prenorm(x3, mod3, k_shift, k_scale):
    B, T, D = x3.shape
    bt = _tile(T, 512, SUBLANE)
    return pl.pallas_call(
        _prenorm_kernel,
        out_shape=jax.ShapeDtypeStruct((B, T, D), BF16),
        grid=(B, T // bt),
        in_specs=[
            pl.BlockSpec((None, bt, D), lambda b, t: (b, t, 0)),
            pl.BlockSpec((None, 1, D), lambda b, t: (b, 0, k_shift)),
            pl.BlockSpec((None, 1, D), lambda b, t: (b, 0, k_scale)),
        ],
        out_specs=pl.BlockSpec((None, bt, D), lambda b, t: (b, t, 0)),
        compiler_params=_params(("parallel", "parallel")),
        name="prenorm",
    )(x3, mod3, mod3)


def _final_norm_kernel(x_ref, w_ref, o_ref):
    x = x_ref[...]
    o_ref[...] = x * lax.rsqrt(jnp.mean(x * x, axis=-1, keepdims=True) + EPS) * w_ref[...]


def _final_norm(x2, w):
    M, D = x2.shape
    bt = _tile(M, 512, SUBLANE)
    return pl.pallas_call(
        _final_norm_kernel,
        out_shape=jax.ShapeDtypeStruct((M, D), F32),
        grid=(M // bt,),
        in_specs=[pl.BlockSpec((bt, D), lambda i: (i, 0)),
                  pl.BlockSpec((1, D), lambda i: (0, 0))],
        out_specs=pl.BlockSpec((bt, D), lambda i: (i, 0)),
        compiler_params=_params(("parallel",)),
        name="final_norm",
    )(x2, w.reshape(1, D))


def _cast_split(shape, steps):
    rows, cols = shape
    nc = 1
    while nc <= steps:
        nr, rem = divmod(steps, nc)
        if (rem == 0 and rows % nr == 0 and cols % nc == 0
                and (rows // nr) % BF16_SUBLANE == 0 and (cols // nc) % LANE == 0):
            return (rows // nr, cols // nc), nc
        nc *= 2
    return None


def _side_casts(side, grid):
    flat = [s.reshape(-1, s.shape[-1]) for s in side]
    splits = [_cast_split(f.shape, int(np.prod(grid))) for f in flat]
    if any(sp is None for sp in splits):
        return None

    def spec(blk, nc):
        def index(*idx):
            step = functools.reduce(lambda acc, t: acc * t[1] + t[0], zip(idx, grid), 0)
            return step // nc, step % nc
        return pl.BlockSpec(blk, index)

    return (flat, [spec(*sp) for sp in splits],
            [jax.ShapeDtypeStruct(f.shape, BF16) for f in flat])


def _cast_blocks(srcs, dsts):
    for src, dst in zip(srcs, dsts):
        dst[...] = src[...].astype(dst.dtype)


def _mm_cast_kernel(a_ref, w_ref, *refs, n_side):
    o_ref = refs[n_side]
    o_ref[...] = _dot(a_ref[...], w_ref[...]).astype(o_ref.dtype)
    _cast_blocks(refs[:n_side], refs[n_side + 1:])


def _matmul(a, w, layer, out_dtype, side=()):
    M, K = a.shape
    N = w.shape[2]
    bm = _tile(M, 1024, SUBLANE)
    bn = _tile(N, 1024, LANE)
    grid = (N // bn, M // bm)
    casts = _side_casts(side, grid)
    if casts is None:
        return _matmul(a, w, layer, out_dtype)[0], [s.astype(BF16) for s in side]
    flat, side_specs, side_shapes = casts
    outs = pl.pallas_call(
        functools.partial(_mm_cast_kernel, n_side=len(flat)),
        out_shape=[jax.ShapeDtypeStruct((M, N), out_dtype)] + side_shapes,
        grid=grid,
        in_specs=[pl.BlockSpec((bm, K), lambda j, i: (i, 0)),
                  pl.BlockSpec((None, K, bn), lambda j, i: (layer, 0, j))] + side_specs,
        out_specs=[pl.BlockSpec((bm, bn), lambda j, i: (i, j))] + side_specs,
        compiler_params=_params(("parallel", "parallel"), V7X_VMEM_BYTES - 4 * 1024 * 1024),
        name="matmul",
    )(a, w, *flat)
    return outs[0], [o.reshape(s.shape) for o, s in zip(outs[1:], side)]


def _mm_res_kernel(a_ref, w_ref, x_ref, g_ref, o_ref):
    o_ref[...] = x_ref[...] + g_ref[...] * _dot(a_ref[...], w_ref[...])


def _matmul_residual(a, w, layer, x2, mod3, k_gate, seq, *, blk):
    M, K = a.shape
    N = w.shape[2]
    bm = _tile(seq, blk, SUBLANE)
    bn = _tile(N, blk, LANE)
    nb = N // bn
    per_seq = seq // bm
    return pl.pallas_call(
        _mm_res_kernel,
        out_shape=jax.ShapeDtypeStruct((M, N), F32),
        grid=(nb, M // bm),
        in_specs=[
            pl.BlockSpec((bm, K), lambda j, i: (i, 0)),
            pl.BlockSpec((None, K, bn), lambda j, i: (layer, 0, j)),
            pl.BlockSpec((bm, bn), lambda j, i: (i, j)),
            pl.BlockSpec((None, 1, bn), lambda j, i: (i // per_seq, 0, k_gate * nb + j)),
        ],
        out_specs=pl.BlockSpec((bm, bn), lambda j, i: (i, j)),
        compiler_params=_params(("parallel", "parallel")),
        name="matmul_residual",
    )(a, w, x2, mod3)


def _out_norm_kernel(a_ref, w_ref, x_ref, g_ref, sh_ref, sc_ref, xo_ref, ho_ref):
    xn = x_ref[...] + g_ref[...] * _dot(a_ref[...], w_ref[...])
    xo_ref[...] = xn
    y = xn * lax.rsqrt(jnp.mean(xn * xn, axis=-1, keepdims=True) + EPS)
    ho_ref[...] = (y * (1.0 + sc_ref[...]) + sh_ref[...]).astype(ho_ref.dtype)


def _out_proj_norm(a, w, layer, x2, mod3, k_gate, k_shift, k_scale, seq, *, bm):
    M, K = a.shape
    N = w.shape[2]
    per_seq = seq // bm

    def mod(k):
        return pl.BlockSpec((None, 1, N), lambda i: (i // per_seq, 0, k))

    return pl.pallas_call(
        _out_norm_kernel,
        out_shape=(jax.ShapeDtypeStruct((M, N), F32), jax.ShapeDtypeStruct((M, N), BF16)),
        grid=(M // bm,),
        in_specs=[
            pl.BlockSpec((bm, K), lambda i: (i, 0)),
            pl.BlockSpec((None, K, N), lambda i: (layer, 0, 0), pipeline_mode=pl.Buffered(1)),
            pl.BlockSpec((bm, N), lambda i: (i, 0)),
            mod(k_gate), mod(k_shift), mod(k_scale),
        ],
        out_specs=(pl.BlockSpec((bm, N), lambda i: (i, 0)), pl.BlockSpec((bm, N), lambda i: (i, 0))),
        compiler_params=_params(("parallel",), V7X_VMEM_BYTES - 4 * 1024 * 1024),
        name="out_proj_norm",
    )(a, w, x2, mod3, mod3, mod3)


def _causal_conv3(p, halo, w):
    row = lax.broadcasted_iota(jnp.int32, p.shape, 0)
    s1 = pltpu.roll(p, 1, axis=0)
    s2 = pltpu.roll(p, 2, axis=0)
    h7 = halo[SUBLANE - 1:SUBLANE, :]
    h6 = halo[SUBLANE - 2:SUBLANE - 1, :]
    s1 = jnp.where(row == 0, h7, s1)
    s2 = jnp.where(row == 0, h6, jnp.where(row == 1, h7, s2))
    return w[0:1, :] * s2 + w[1:2, :] * s1 + w[2:3, :] * p


def _sconv_in_kernel(a_ref, wb_ref, wc_ref, wx_ref, cw_ref, o_ref, halo_ref, *, per_seq):
    i = pl.program_id(1)
    a = a_ref[...]
    p = _dot(a, wc_ref[...]) * _dot(a, wx_ref[...])
    halo = halo_ref[...]
    halo = jnp.where(i % per_seq == 0, jnp.zeros_like(halo), halo)
    halo_ref[...] = p[p.shape[0] - SUBLANE:, :]
    o_ref[...] = (_dot(a, wb_ref[...]) * _causal_conv3(p, halo, cw_ref[...])).astype(o_ref.dtype)


def _sconv_in(a, w_in, layer, conv_w3, seq):
    M, D = a.shape
    bm = _tile(seq, 1024, SUBLANE)
    bn = _tile(D, 512, LANE)
    nb = D // bn
    return pl.pallas_call(
        functools.partial(_sconv_in_kernel, per_seq=seq // bm),
        out_shape=jax.ShapeDtypeStruct((M, D), BF16),
        grid=(nb, M // bm),
        in_specs=[
            pl.BlockSpec((bm, D), lambda j, i: (i, 0)),
            pl.BlockSpec((None, D, bn), lambda j, i: (layer, 0, j)),
            pl.BlockSpec((None, D, bn), lambda j, i: (layer, 0, nb + j)),
            pl.BlockSpec((None, D, bn), lambda j, i: (layer, 0, 2 * nb + j)),
            pl.BlockSpec((3, bn), lambda j, i: (0, j)),
        ],
        out_specs=pl.BlockSpec((bm, bn), lambda j, i: (i, j)),
        scratch_shapes=[pltpu.VMEM((SUBLANE, bn), F32)],
        compiler_params=_params(("arbitrary", "arbitrary")),
        name="sconv_in",
    )(a, w_in, w_in, w_in, conv_w3)


def _ffn_up_kernel(a_ref, wa_ref, wu0_ref, wu1_ref, cw_ref, cb_ref, o_ref, halo_ref,
                   *, per_seq, ragged):
    j = pl.program_id(0)
    i = pl.program_id(1)
    bn = o_ref.shape[1]
    half = bn // 2

    def compute(width):
        a = a_ref[...]
        p = _dot(a, wa_ref[:, :width])
        halo = halo_ref[:, :width]
        halo = jnp.where(i % per_seq == 0, jnp.zeros_like(halo), halo)
        halo_ref[:, :width] = p[p.shape[0] - SUBLANE:, :]
        gate = _silu(_causal_conv3(p, halo, cw_ref[:, :width]) + cb_ref[:, :width])
        u = _dot(a, wu0_ref[...])
        if width == bn:
            u = jnp.concatenate([u, _dot(a, wu1_ref[...])], axis=1)
        o_ref[:, :width] = (gate * u).astype(o_ref.dtype)

    if ragged:
        last = pl.num_programs(0) - 1
        pl.when(j < last)(lambda: compute(bn))
        pl.when(j == last)(lambda: compute(half))
    else:
        compute(bn)


def _ffn_up(a, w_up, layer, conv_w3, conv_b, seq):
    M, D = a.shape
    F = w_up.shape[2] // 2
    bm = _tile(seq, 1024, SUBLANE)
    bn = FFN_BN
    half = bn // 2
    assert F % half == 0
    nb = pl.cdiv(F, bn)
    u0 = F // half
    last = 2 * F // half - 1
    return pl.pallas_call(
        functools.partial(_ffn_up_kernel, per_seq=seq // bm, ragged=F % bn != 0),
        out_shape=jax.ShapeDtypeStruct((M, F), BF16),
        grid=(nb, M // bm),
        in_specs=[
            pl.BlockSpec((bm, D), lambda j, i: (i, 0)),
            pl.BlockSpec((None, D, bn), lambda j, i: (layer, 0, j)),
            pl.BlockSpec((None, D, half), lambda j, i: (layer, 0, u0 + 2 * j)),
            pl.BlockSpec((None, D, half), lambda j, i: (layer, 0, jnp.minimum(u0 + 2 * j + 1, last))),
            pl.BlockSpec((3, bn), lambda j, i: (0, j)),
            pl.BlockSpec((1, bn), lambda j, i: (0, j)),
        ],
        out_specs=pl.BlockSpec((bm, bn), lambda j, i: (i, j)),
        scratch_shapes=[pltpu.VMEM((SUBLANE, bn), F32)],
        compiler_params=_params(("arbitrary", "arbitrary")),
        name="ffn_up",
    )(a, w_up, w_up, w_up, conv_w3, conv_b)


DIAG = -2
N_FINE = 3
LOG_SPLIT = 2


def _hgrn_constants(C, W):
    u = np.arange(C)[None, :]
    t = np.arange(C)[:, None]
    bands = [u <= t]
    masks = []
    for l in range(1, N_FINE):
        h = 1 << l
        m = (t // (2 * h)) * (2 * h) + h
        upper = t >= m
        bands.append(np.where(upper, (u > m) & (u <= t), (u > t) & (u <= m)))
        masks.append(np.broadcast_to(upper[:SUBLANE], (SUBLANE, W)))
    d = np.concatenate(bands, axis=0).astype(np.float32)
    dmat = np.concatenate([d] * LOG_SPLIT, axis=1)
    rowmask = np.concatenate(masks, axis=0).astype(np.float32)
    s = np.arange(2 * C)[None, :] % C
    x = t ^ s
    lev = np.where(t > s, np.floor(np.log2(np.maximum(x, 1))), np.where(t == s, DIAG, -1))
    return jnp.asarray(dmat, BF16), jnp.asarray(rowmask, F32), jnp.asarray(lev, jnp.int32)


def _hgrn_kernel(q_ref, f_ref, v_ref, g_ref, lbl_ref, gn_ref, dmat_ref, rmask_ref, lev_ref,
                 *refs, n_chunks, slot, group, n_side):
    o_ref, st_ref = refs[n_side], refs[-1]
    _cast_blocks(refs[:n_side], refs[n_side + 1:-1])
    C = CHUNK
    W = 2 * HEAD
    n_lvl = int(np.log2(C))

    @pl.when(pl.program_id(2) == 0)
    def _():
        st_ref[...] = jnp.zeros_like(st_ref)

    rows = [lbl_ref[k:k + 1, :] for k in range(lbl_ref.shape[0])]
    mx = functools.reduce(jnp.maximum, rows)
    ex = [jnp.exp(r - mx) for r in rows]
    lb = functools.reduce(jnp.add, ex[:slot + 1]) / functools.reduce(jnp.add, ex)
    gn = gn_ref[...]
    zc = jnp.zeros((C, HEAD), BF16)
    zs = jnp.zeros((HEAD, HEAD), BF16)

    def by_head(x):
        return jnp.concatenate(
            [jnp.concatenate([x[:, :HEAD], zc], axis=1),
             jnp.concatenate([zc, x[:, HEAD:]], axis=1)], axis=0)

    G = group
    R = G * C

    def per_chunk(x):
        return [x[c * C:(c + 1) * C] for c in range(G)]

    def body(gi, carry):
        rs = pl.ds(pl.multiple_of(gi * R, R), R)
        q_all = q_ref[rs, :]
        f_all = lb + (1.0 - lb) * jax.nn.sigmoid(f_ref[rs, :])
        k_all = 1.0 - f_all
        lg = jnp.log(f_all) * LOG2E
        pieces, rem = [], lg
        for _ in range(LOG_SPLIT - 1):
            pieces.append(rem.astype(BF16))
            rem = rem - pieces[-1].astype(F32)
        pieces.append(rem.astype(BF16))
        q, k, f = per_chunk(q_all), per_chunk(k_all), per_chunk(f_all)
        vb = per_chunk(v_ref[rs, :].astype(BF16))
        dmat = dmat_ref[...]
        args = [_dot(dmat, jnp.concatenate(list(xs), axis=0))
                for xs in zip(*[per_chunk(x) for x in pieces])]
        b = [x[0:C] for x in args]
        b_last = [x[C - 1:C, :] for x in b]
        e_fine = [jnp.exp2(x[C:]) for x in args]
        lev = lev_ref[...]

        a = [0.0] * G
        for l in range(1, N_FINE):
            up = jnp.tile(rmask_ref[(l - 1) * SUBLANE:l * SUBLANE, :], (C // SUBLANE, 1)) != 0.0
            for c in range(G):
                z = (jnp.where(up, q[c], k[c]) * e_fine[c][(l - 1) * C:l * C]).astype(BF16)
                a[c] = jnp.where(lev == l, _dot_nt(z, by_head(z)), a[c])
        lo = 1 << N_FINE
        qh = [[q[c][:lo] * jnp.exp2(b[c][:lo])] for c in range(G)]
        kh = [[k[c][C - lo:] * jnp.exp2(b_last[c] - b[c][C - lo:])] for c in range(G)]
        for l in range(N_FINE, n_lvl):
            h = 1 << l
            zh = jnp.zeros((h, W), BF16)
            for c in range(G):
                qe, ke = [], []
                for base in range(0, C, 2 * h):
                    m = base + h
                    bm = b[c][m:m + 1, :]
                    qp = q[c][m:m + h] * jnp.exp2(b[c][m:m + h] - bm)
                    kp = k[c][base:m] * jnp.exp2(bm - b[c][base:m])
                    if base == 0:
                        qh[c].append(qp * jnp.exp2(bm))
                    if base == C - 2 * h:
                        kh[c].insert(0, kp * jnp.exp2(b_last[c] - bm))
                    qe += [zh, qp.astype(BF16)]
                    ke += [kp.astype(BF16), zh]
                p = _dot_nt(jnp.concatenate(qe, axis=0), by_head(jnp.concatenate(ke, axis=0)))
                a[c] = jnp.where(lev == l, p, a[c])
        qh = [jnp.concatenate(x, axis=0).astype(BF16) for x in qh]
        kh = [jnp.concatenate(x, axis=0).astype(BF16) for x in kh]

        def head_sums(x):
            return jnp.concatenate(
                [jnp.broadcast_to(jnp.sum(x[:, :HEAD], axis=-1, keepdims=True), (C, C)),
                 jnp.broadcast_to(jnp.sum(x[:, HEAD:], axis=-1, keepdims=True), (C, C))], axis=1)

        for c in range(G):
            a[c] = jnp.where(lev == DIAG, head_sums(q[c] * k[c]), a[c])
            a[c] = jnp.where(lev == 0, head_sums(q[c] * f[c] * pltpu.roll(k[c], 1, axis=0)), a[c])

        upd0 = [_dot_tn(vb[c][:, :HEAD], kh[c][:, :HEAD]) for c in range(G)]
        upd1 = [_dot_tn(vb[c][:, HEAD:], kh[c][:, HEAD:]) for c in range(G)]
        o_intra = [_dot(a[c].astype(BF16), by_head(vb[c])) for c in range(G)]
        d_s = [jnp.exp2(x) for x in b_last]
        s0 = st_ref[0]
        s1 = st_ref[1]
        o = []
        for c in range(G):
            s_bd = jnp.concatenate(
                [jnp.concatenate([s0.astype(BF16), zs], axis=1),
                 jnp.concatenate([zs, s1.astype(BF16)], axis=1)], axis=0)
            o.append(_dot_nt(qh[c], s_bd) + o_intra[c])
            s0 = s0 * d_s[c][:, :HEAD] + upd0[c]
            s1 = s1 * d_s[c][:, HEAD:] + upd1[c]
        st_ref[0] = s0
        st_ref[1] = s1

        o_all = jnp.concatenate(o, axis=0)
        sq = o_all * o_all
        ms = jnp.concatenate(
            [jnp.broadcast_to(jnp.mean(sq[:, :HEAD], axis=-1, keepdims=True), (R, HEAD)),
             jnp.broadcast_to(jnp.mean(sq[:, HEAD:], axis=-1, keepdims=True), (R, HEAD))], axis=1)
        g = g_ref[rs, :]
        o_ref[rs, :] = (o_all * lax.rsqrt(ms + EPS) * gn * _silu(g)).astype(o_ref.dtype)
        return carry

    lax.fori_loop(0, n_chunks // G, body, 0)


def _hgrn_recurrence(proj, lb_logits, g_norm, batch, seq, slot, side=()):
    M, D4 = proj.shape
    D = D4 // 4
    W = 2 * HEAD
    npair = D // W
    tb = _tile(seq, 2048, CHUNK)
    nt = seq // tb
    group = _tile(tb // CHUNK, 8, 1)
    consts = _hgrn_constants(CHUNK, W)

    def col(part):
        return pl.BlockSpec((tb, W), lambda b, p, t: (b * nt + t, part * npair + p))

    def whole(arr):
        return pl.BlockSpec(arr.shape, lambda b, p, t: (0, 0))

    grid = (batch, npair, nt)
    casts = _side_casts(side, grid)
    if casts is None:
        return (_hgrn_recurrence(proj, lb_logits, g_norm, batch, seq, slot)[0],
                [s.astype(BF16) for s in side])
    flat, side_specs, side_shapes = casts
    outs = pl.pallas_call(
        functools.partial(_hgrn_kernel, n_chunks=tb // CHUNK, slot=slot, group=group,
                          n_side=len(flat)),
        out_shape=[jax.ShapeDtypeStruct((M, D), BF16)] + side_shapes,
        grid=grid,
        in_specs=[col(0), col(1), col(2), col(3),
                  pl.BlockSpec((lb_logits.shape[0], W), lambda b, p, t: (0, p)),
                  pl.BlockSpec((1, W), lambda b, p, t: (0, p))]
        + [whole(c) for c in consts] + side_specs,
        out_specs=[pl.BlockSpec((tb, W), lambda b, p, t: (b * nt + t, p))] + side_specs,
        scratch_shapes=[pltpu.VMEM((2, HEAD, HEAD), F32)],
        compiler_params=_params(("parallel", "parallel", "arbitrary")),
        name="hgrn_recurrence",
    )(proj, proj, proj, proj, lb_logits, g_norm.reshape(1, D), *consts, *flat)
    return outs[0], [o.reshape(s.shape) for o, s in zip(outs[1:], side)]


def kernel(x, c, w_cond, b_cond, ada_w, ada_b, lb_logits, hgrn_w_in, hgrn_norm, hgrn_w_out,
           sconv_w_in, sconv_conv_w, sconv_w_out, ffn_w_up, ffn_conv_w, ffn_conv_b,
           ffn_w_down, final_norm):
    B, T, D = x.shape
    depth = ada_w.shape[0]
    F = ffn_conv_b.shape[1]
    M = B * T
    n_mixers = 2

    c_emb = _cond_matmul(c, w_cond[None], b_cond[None], act=True)[0]
    mod = _cond_matmul(c_emb, ada_w, ada_b, act=False)

    w_hin = hgrn_w_in.astype(BF16)

    x2 = x.reshape(M, D)
    for i in range(depth):
        mod3 = mod[i].reshape(B, 1, N_MOD * D)
        j = i // n_mixers
        hm = _prenorm(x2.reshape(B, T, D), mod3, 0, 1).reshape(M, D)
        if i % n_mixers == 0:
            first = i == 0
            proj, copies = _matmul(hm, w_hin, j, F32, side=(hgrn_w_out, ffn_w_up) if first else ())
            if first:
                w_hout, w_up = copies
            y, copies = _hgrn_recurrence(proj, lb_logits, hgrn_norm[j], B, T, slot=i,
                                         side=(sconv_w_in, sconv_w_out, ffn_w_down) if first else ())
            if first:
                w_sin, w_sout, w_down = copies
            x2, hf = _out_proj_norm(y, w_hout, j, x2, mod3, 2, 3, 4, T, bm=OUT_BM)
        else:
            y = _sconv_in(hm, w_sin, j, sconv_conv_w[j].T, T)
            x2, hf = _out_proj_norm(y, w_sout, j, x2, mod3, 2, 3, 4, T, bm=OUT_BM)

        h = _ffn_up(hf, w_up, i, ffn_conv_w[i].T, ffn_conv_b[i].reshape(1, F), T)
        x2 = _matmul_residual(h, w_down, i, x2, mod3, 5, T, blk=512)
    return _final_norm(x2, final_norm).reshape(B, T, D)
```

```python
import functools

import numpy as np
import jax
import jax.numpy as jnp
from jax import lax
from jax.experimental import pallas as pl
from jax.experimental.pallas import tpu as pltpu

EPS = 1e-6
CHUNK = 64
HEAD = 128
N_MOD = 6
LOG2E = 1.4426950408889634
F32 = jnp.float32
BF16 = jnp.bfloat16

V7X_VMEM_BYTES = 64 * 1024 * 1024
VMEM_LIMIT = V7X_VMEM_BYTES - 8 * 1024 * 1024
LANE = 128
SUBLANE = 8
BF16_SUBLANE = 16
FFN_BN = 512
OUT_BM = 256


def _tile(n, pref, align):
    t = min(pref, n)
    t -= t % align
    while t >= align:
        if n % t == 0:
            return t
        t -= align
    return n


def _params(sem, vmem=VMEM_LIMIT):
    return pltpu.CompilerParams(dimension_semantics=sem, vmem_limit_bytes=vmem)


def _dot(a, b):
    return jnp.dot(a, b, preferred_element_type=F32)


def _dot_nt(a, b):
    return lax.dot_general(a, b, (((1,), (1,)), ((), ())), preferred_element_type=F32)


def _dot_tn(a, b):
    return lax.dot_general(a, b, (((0,), (0,)), ((), ())), preferred_element_type=F32)


def _silu(x):
    return x * jax.nn.sigmoid(x)


def _cond_kernel(c_ref, w_ref, b_ref, o_ref, *, act):
    acc = _dot(c_ref[...].astype(BF16), w_ref[...].astype(BF16)) + b_ref[...]
    o_ref[...] = _silu(acc) if act else acc


def _cond_matmul(c, w, b, *, act, first=0, layers=None):
    _, K, N = w.shape
    L = w.shape[0] - first if layers is None else layers
    B = c.shape[0]
    bn = _tile(N, 512, LANE)
    return pl.pallas_call(
        functools.partial(_cond_kernel, act=act),
        out_shape=jax.ShapeDtypeStruct((L, B, N), F32),
        grid=(L, N // bn),
        in_specs=[
            pl.BlockSpec((B, K), lambda l, j: (0, 0)),
            pl.BlockSpec((None, K, bn), lambda l, j: (first + l, 0, j)),
            pl.BlockSpec((None, 1, bn), lambda l, j: (first + l, 0, j)),
        ],
        out_specs=pl.BlockSpec((None, B, bn), lambda l, j: (l, 0, j)),
        compiler_params=_params(("parallel", "parallel")),
        name="cond_matmul",
    )(c, w, b.reshape(b.shape[0], 1, N))


def _prenorm_kernel(x_ref, sh_ref, sc_ref, o_ref):
    x = x_ref[...]
    y = x * lax.rsqrt(jnp.mean(x * x, axis=-1, keepdims=True) + EPS)
    o_ref[...] = (y * (1.0 + sc_ref[...]) + sh_ref[...]).astype(o_ref.dtype)


def _prenorm(x3, mod3, k_shift, k_scale):
    B, T, D = x3.shape
    bt = _tile(T, 512, SUBLANE)
    return pl.pallas_call(
        _prenorm_kernel,
        out_shape=jax.ShapeDtypeStruct((B, T, D), BF16),
        grid=(B, T // bt),
        in_specs=[
            pl.BlockSpec((None, bt, D), lambda b, t: (b, t, 0)),
            pl.BlockSpec((None, 1, D), lambda b, t: (b, 0, k_shift)),
            pl.BlockSpec((None, 1, D), lambda b, t: (b, 0, k_scale)),
        ],
        out_specs=pl.BlockSpec((None, bt, D), lambda b, t: (b, t, 0)),
        compiler_params=_params(("parallel", "parallel")),
        name="prenorm",
    )(x3, mod3, mod3)


def _final_norm_kernel(x_ref, w_ref, o_ref):
    x = x_ref[...]
    o_ref[...] = x * lax.rsqrt(jnp.mean(x * x, axis=-1, keepdims=True) + EPS) * w_ref[...]


def _final_norm(x2, w):
    M, D = x2.shape
    bt = _tile(M, 512, SUBLANE)
    return pl.pallas_call(
        _final_norm_kernel,
        out_shape=jax.ShapeDtypeStruct((M, D), F32),
        grid=(M // bt,),
        in_specs=[pl.BlockSpec((bt, D), lambda i: (i, 0)),
                  pl.BlockSpec((1, D), lambda i: (0, 0))],
        out_specs=pl.BlockSpec((bt, D), lambda i: (i, 0)),
        compiler_params=_params(("parallel",)),
        name="final_norm",
    )(x2, w.reshape(1, D))


def _cast_split(shape, steps):
    rows, cols = shape
    nc = 1
    while nc <= steps:
        nr, rem = divmod(steps, nc)
        if (rem == 0 and rows % nr == 0 and cols % nc == 0
                and (rows // nr) % BF16_SUBLANE == 0 and (cols // nc) % LANE == 0):
            return (rows // nr, cols // nc), nc
        nc *= 2
    return None


def _side_casts(side, grid):
    flat = [s.reshape(-1, s.shape[-1]) for s in side]
    splits = [_cast_split(f.shape, int(np.prod(grid))) for f in flat]
    if any(sp is None for sp in splits):
        return None

    def spec(blk, nc):
        def index(*idx):
            step = functools.reduce(lambda acc, t: acc * t[1] + t[0], zip(idx, grid), 0)
            return step // nc, step % nc
        return pl.BlockSpec(blk, index)

    return (flat, [spec(*sp) for sp in splits],
            [jax.ShapeDtypeStruct(f.shape, BF16) for f in flat])


def _cast_blocks(srcs, dsts):
    for src, dst in zip(srcs, dsts):
        dst[...] = src[...].astype(dst.dtype)


def _mm_cast_kernel(a_ref, w_ref, *refs, n_side):
    o_ref = refs[n_side]
    o_ref[...] = _dot(a_ref[...], w_ref[...]).astype(o_ref.dtype)
    _cast_blocks(refs[:n_side], refs[n_side + 1:])


def _matmul(a, w, layer, out_dtype, side=()):
    M, K = a.shape
    N = w.shape[2]
    bm = _tile(M, 1024, SUBLANE)
    bn = _tile(N, 1024, LANE)
    grid = (N // bn, M // bm)
    casts = _side_casts(side, grid)
    if casts is None:
        return _matmul(a, w, layer, out_dtype)[0], [s.astype(BF16) for s in side]
    flat, side_specs, side_shapes = casts
    outs = pl.pallas_call(
        functools.partial(_mm_cast_kernel, n_side=len(flat)),
        out_shape=[jax.ShapeDtypeStruct((M, N), out_dtype)] + side_shapes,
        grid=grid,
        in_specs=[pl.BlockSpec((bm, K), lambda j, i: (i, 0)),
                  pl.BlockSpec((None, K, bn), lambda j, i: (layer, 0, j))] + side_specs,
        out_specs=[pl.BlockSpec((bm, bn), lambda j, i: (i, j))] + side_specs,
        compiler_params=_params(("parallel", "parallel"), V7X_VMEM_BYTES - 4 * 1024 * 1024),
        name="matmul",
    )(a, w, *flat)
    return outs[0], [o.reshape(s.shape) for o, s in zip(outs[1:], side)]


def _mm_res_kernel(a_ref, w_ref, x_ref, g_ref, o_ref):
    o_ref[...] = x_ref[...] + g_ref[...] * _dot(a_ref[...], w_ref[...])


def _matmul_residual(a, w, layer, x2, mod3, k_gate, seq, *, blk):
    M, K = a.shape
    N = w.shape[2]
    bm = _tile(seq, blk, SUBLANE)
    bn = _tile(N, blk, LANE)
    nb = N // bn
    per_seq = seq // bm
    return pl.pallas_call(
        _mm_res_kernel,
        out_shape=jax.ShapeDtypeStruct((M, N), F32),
        grid=(nb, M // bm),
        in_specs=[
            pl.BlockSpec((bm, K), lambda j, i: (i, 0)),
            pl.BlockSpec((None, K, bn), lambda j, i: (layer, 0, j)),
            pl.BlockSpec((bm, bn), lambda j, i: (i, j)),
            pl.BlockSpec((None, 1, bn), lambda j, i: (i // per_seq, 0, k_gate * nb + j)),
        ],
        out_specs=pl.BlockSpec((bm, bn), lambda j, i: (i, j)),
        compiler_params=_params(("parallel", "parallel")),
        name="matmul_residual",
    )(a, w, x2, mod3)


def _out_norm_kernel(a_ref, w_ref, x_ref, g_ref, sh_ref, sc_ref, xo_ref, ho_ref):
    xn = x_ref[...] + g_ref[...] * _dot(a_ref[...], w_ref[...])
    xo_ref[...] = xn
    y = xn * lax.rsqrt(jnp.mean(xn * xn, axis=-1, keepdims=True) + EPS)
    ho_ref[...] = (y * (1.0 + sc_ref[...]) + sh_ref[...]).astype(ho_ref.dtype)


def _out_proj_norm(a, w, layer, x2, mod3, k_gate, k_shift, k_scale, seq, *, bm):
    M, K = a.shape
    N = w.shape[2]
    per_seq = seq // bm

    def mod(k):
        return pl.BlockSpec((None, 1, N), lambda i: (i // per_seq, 0, k))

    return pl.pallas_call(
        _out_norm_kernel,
        out_shape=(jax.ShapeDtypeStruct((M, N), F32), jax.ShapeDtypeStruct((M, N), BF16)),
        grid=(M // bm,),
        in_specs=[
            pl.BlockSpec((bm, K), lambda i: (i, 0)),
            pl.BlockSpec((None, K, N), lambda i: (layer, 0, 0), pipeline_mode=pl.Buffered(1)),
            pl.BlockSpec((bm, N), lambda i: (i, 0)),
            mod(k_gate), mod(k_shift), mod(k_scale),
        ],
        out_specs=(pl.BlockSpec((bm, N), lambda i: (i, 0)), pl.BlockSpec((bm, N), lambda i: (i, 0))),
        compiler_params=_params(("parallel",), V7X_VMEM_BYTES - 4 * 1024 * 1024),
        name="out_proj_norm",
    )(a, w, x2, mod3, mod3, mod3)


def _causal_conv3(p, halo, w):
    row = lax.broadcasted_iota(jnp.int32, p.shape, 0)
    s1 = pltpu.roll(p, 1, axis=0)
    s2 = pltpu.roll(p, 2, axis=0)
    h7 = halo[SUBLANE - 1:SUBLANE, :]
    h6 = halo[SUBLANE - 2:SUBLANE - 1, :]
    s1 = jnp.where(row == 0, h7, s1)
    s2 = jnp.where(row == 0, h6, jnp.where(row == 1, h7, s2))
    return w[0:1, :] * s2 + w[1:2, :] * s1 + w[2:3, :] * p


def _sconv_in_kernel(a_ref, wb_ref, wc_ref, wx_ref, cw_ref, o_ref, halo_ref, *, per_seq):
    i = pl.program_id(1)
    a = a_ref[...]
    p = _dot(a, wc_ref[...]) * _dot(a, wx_ref[...])
    halo = halo_ref[...]
    halo = jnp.where(i % per_seq == 0, jnp.zeros_like(halo), halo)
    halo_ref[...] = p[p.shape[0] - SUBLANE:, :]
    o_ref[...] = (_dot(a, wb_ref[...]) * _causal_conv3(p, halo, cw_ref[...])).astype(o_ref.dtype)


def _sconv_in(a, w_in, layer, conv_w3, seq):
    M, D = a.shape
    bm = _tile(seq, 1024, SUBLANE)
    bn = _tile(D, 512, LANE)
    nb = D // bn
    return pl.pallas_call(
        functools.partial(_sconv_in_kernel, per_seq=seq // bm),
        out_shape=jax.ShapeDtypeStruct((M, D), BF16),
        grid=(nb, M // bm),
        in_specs=[
            pl.BlockSpec((bm, D), lambda j, i: (i, 0)),
            pl.BlockSpec((None, D, bn), lambda j, i: (layer, 0, j)),
            pl.BlockSpec((None, D, bn), lambda j, i: (layer, 0, nb + j)),
            pl.BlockSpec((None, D, bn), lambda j, i: (layer, 0, 2 * nb + j)),
            pl.BlockSpec((3, bn), lambda j, i: (0, j)),
        ],
        out_specs=pl.BlockSpec((bm, bn), lambda j, i: (i, j)),
        scratch_shapes=[pltpu.VMEM((SUBLANE, bn), F32)],
        compiler_params=_params(("arbitrary", "arbitrary")),
        name="sconv_in",
    )(a, w_in, w_in, w_in, conv_w3)


def _ffn_up_kernel(a_ref, wa_ref, wu0_ref, wu1_ref, cw_ref, cb_ref, *refs, per_seq, ragged):
    if len(refs) == 2:
        (o_ref, halo_ref), job = refs, None
    else:
        ce_ref, aw_ref, ab_ref, o_ref, mo_ref, halo_ref = refs
        job = True
    j = pl.program_id(0)
    i = pl.program_id(1)
    bn = o_ref.shape[1]
    half = bn // 2

    def compute(width):
        if job:
            mo_ref[...] = _dot(ce_ref[...].astype(BF16), aw_ref[...].astype(BF16)) + ab_ref[...]
        a = a_ref[...]
        p = _dot(a, wa_ref[:, :width])
        halo = halo_ref[:, :width]
        halo = jnp.where(i % per_seq == 0, jnp.zeros_like(halo), halo)
        halo_ref[:, :width] = p[p.shape[0] - SUBLANE:, :]
        gate = _silu(_causal_conv3(p, halo, cw_ref[:, :width]) + cb_ref[:, :width])
        u = _dot(a, wu0_ref[...])
        if width == bn:
            u = jnp.concatenate([u, _dot(a, wu1_ref[...])], axis=1)
        o_ref[:, :width] = (gate * u).astype(o_ref.dtype)

    if ragged:
        last = pl.num_programs(0) - 1
        pl.when(j < last)(lambda: compute(bn))
        pl.when(j == last)(lambda: compute(half))
    else:
        compute(bn)


def _ffn_up(a, w_up, layer, conv_w3, conv_b, seq, mod_job=None):
    M, D = a.shape
    F = w_up.shape[2] // 2
    bm = _tile(seq, 1024, SUBLANE)
    bn = FFN_BN
    half = bn // 2
    assert F % half == 0
    nb = pl.cdiv(F, bn)
    u0 = F // half
    last = 2 * F // half - 1
    ni = M // bm
    in_specs = [
        pl.BlockSpec((bm, D), lambda j, i: (i, 0)),
        pl.BlockSpec((None, D, bn), lambda j, i: (layer, 0, j)),
        pl.BlockSpec((None, D, half), lambda j, i: (layer, 0, u0 + 2 * j)),
        pl.BlockSpec((None, D, half), lambda j, i: (layer, 0, jnp.minimum(u0 + 2 * j + 1, last))),
        pl.BlockSpec((3, bn), lambda j, i: (0, j)),
        pl.BlockSpec((1, bn), lambda j, i: (0, j)),
    ]
    out_shape = [jax.ShapeDtypeStruct((M, F), BF16)]
    out_specs = [pl.BlockSpec((bm, bn), lambda j, i: (i, j))]
    args = [a, w_up, w_up, w_up, conv_w3, conv_b]
    if mod_job is not None:
        c_emb, ada_w, ada_b, nxt = mod_job
        B, n_mod = c_emb.shape[0], ada_w.shape[2]
        n_tiles = n_mod // LANE
        if nb * ni < n_tiles:
            return (_ffn_up(a, w_up, layer, conv_w3, conv_b, seq),
                    _cond_matmul(c_emb, ada_w, ada_b, act=False, first=nxt, layers=1)[0])

        def tile_of(j, i):
            return jnp.minimum(j * ni + i, n_tiles - 1)

        in_specs += [pl.BlockSpec(c_emb.shape, lambda j, i: (0, 0)),
                     pl.BlockSpec((None, D, LANE), lambda j, i: (nxt, 0, tile_of(j, i))),
                     pl.BlockSpec((None, 1, LANE), lambda j, i: (nxt, 0, tile_of(j, i)))]
        out_shape.append(jax.ShapeDtypeStruct((B, n_mod), F32))
        out_specs.append(pl.BlockSpec((B, LANE), lambda j, i: (0, tile_of(j, i))))
        args += [c_emb, ada_w, ada_b.reshape(ada_b.shape[0], 1, n_mod)]
    outs = pl.pallas_call(
        functools.partial(_ffn_up_kernel, per_seq=seq // bm, ragged=F % bn != 0),
        out_shape=out_shape,
        grid=(nb, ni),
        in_specs=in_specs,
        out_specs=out_specs,
        scratch_shapes=[pltpu.VMEM((SUBLANE, bn), F32)],
        compiler_params=_params(("arbitrary", "arbitrary")),
        name="ffn_up",
    )(*args)
    return outs if mod_job is not None else outs[0]


DIAG = -2
N_FINE = 3
LOG_SPLIT = 2


def _hgrn_constants(C, W):
    u = np.arange(C)[None, :]
    t = np.arange(C)[:, None]
    bands = [u <= t]
    masks = []
    for l in range(1, N_FINE):
        h = 1 << l
        m = (t // (2 * h)) * (2 * h) + h
        upper = t >= m
        bands.append(np.where(upper, (u > m) & (u <= t), (u > t) & (u <= m)))
        masks.append(np.broadcast_to(upper[:SUBLANE], (SUBLANE, W)))
    d = np.concatenate(bands, axis=0).astype(np.float32)
    dmat = np.concatenate([d] * LOG_SPLIT, axis=1)
    rowmask = np.concatenate(masks, axis=0).astype(np.float32)
    s = np.arange(2 * C)[None, :] % C
    x = t ^ s
    lev = np.where(t > s, np.floor(np.log2(np.maximum(x, 1))), np.where(t == s, DIAG, -1))
    return jnp.asarray(dmat, BF16), jnp.asarray(rowmask, F32), jnp.asarray(lev, jnp.int32)


def _hgrn_kernel(q_ref, f_ref, v_ref, g_ref, lbl_ref, gn_ref, dmat_ref, rmask_ref, lev_ref,
                 *refs, n_chunks, slot, group, n_side):
    o_ref, st_ref = refs[n_side], refs[-1]
    _cast_blocks(refs[:n_side], refs[n_side + 1:-1])
    C = CHUNK
    W = 2 * HEAD
    n_lvl = int(np.log2(C))

    @pl.when(pl.program_id(2) == 0)
    def _():
        st_ref[...] = jnp.zeros_like(st_ref)

    rows = [lbl_ref[k:k + 1, :] for k in range(lbl_ref.shape[0])]
    mx = functools.reduce(jnp.maximum, rows)
    ex = [jnp.exp(r - mx) for r in rows]
    lb = functools.reduce(jnp.add, ex[:slot + 1]) / functools.reduce(jnp.add, ex)
    gn = gn_ref[...]
    zc = jnp.zeros((C, HEAD), BF16)
    zs = jnp.zeros((HEAD, HEAD), BF16)

    def by_head(x):
        return jnp.concatenate(
            [jnp.concatenate([x[:, :HEAD], zc], axis=1),
             jnp.concatenate([zc, x[:, HEAD:]], axis=1)], axis=0)

    G = group
    R = G * C

    def per_chunk(x):
        return [x[c * C:(c + 1) * C] for c in range(G)]

    def body(gi, carry):
        rs = pl.ds(pl.multiple_of(gi * R, R), R)
        q_all = q_ref[rs, :]
        f_all = lb + (1.0 - lb) * jax.nn.sigmoid(f_ref[rs, :])
        k_all = 1.0 - f_all
        lg = jnp.log(f_all) * LOG2E
        pieces, rem = [], lg
        for _ in range(LOG_SPLIT - 1):
            pieces.append(rem.astype(BF16))
            rem = rem - pieces[-1].astype(F32)
        pieces.append(rem.astype(BF16))
        q, k, f = per_chunk(q_all), per_chunk(k_all), per_chunk(f_all)
        vb = per_chunk(v_ref[rs, :].astype(BF16))
        dmat = dmat_ref[...]
        args = [_dot(dmat, jnp.concatenate(list(xs), axis=0))
                for xs in zip(*[per_chunk(x) for x in pieces])]
        b = [x[0:C] for x in args]
        b_last = [x[C - 1:C, :] for x in b]
        e_fine = [jnp.exp2(x[C:]) for x in args]
        lev = lev_ref[...]

        a = [0.0] * G
        for l in range(1, N_FINE):
            up = jnp.tile(rmask_ref[(l - 1) * SUBLANE:l * SUBLANE, :], (C // SUBLANE, 1)) != 0.0
            for c in range(G):
                z = (jnp.where(up, q[c], k[c]) * e_fine[c][(l - 1) * C:l * C]).astype(BF16)
                a[c] = jnp.where(lev == l, _dot_nt(z, by_head(z)), a[c])
        lo = 1 << N_FINE
        qh = [[q[c][:lo] * jnp.exp2(b[c][:lo])] for c in range(G)]
        kh = [[k[c][C - lo:] * jnp.exp2(b_last[c] - b[c][C - lo:])] for c in range(G)]
        for l in range(N_FINE, n_lvl):
            h = 1 << l
            zh = jnp.zeros((h, W), BF16)
            for c in range(G):
                qe, ke = [], []
                for base in range(0, C, 2 * h):
                    m = base + h
                    bm = b[c][m:m + 1, :]
                    qp = q[c][m:m + h] * jnp.exp2(b[c][m:m + h] - bm)
                    kp = k[c][base:m] * jnp.exp2(bm - b[c][base:m])
                    if base == 0:
                        qh[c].append(qp * jnp.exp2(bm))
                    if base == C - 2 * h:
                        kh[c].insert(0, kp * jnp.exp2(b_last[c] - bm))
                    qe += [zh, qp.astype(BF16)]
                    ke += [kp.astype(BF16), zh]
                p = _dot_nt(jnp.concatenate(qe, axis=0), by_head(jnp.concatenate(ke, axis=0)))
                a[c] = jnp.where(lev == l, p, a[c])
        qh = [jnp.concatenate(x, axis=0).astype(BF16) for x in qh]
        kh = [jnp.concatenate(x, axis=0).astype(BF16) for x in kh]

        def head_sums(x):
            return jnp.concatenate(
                [jnp.broadcast_to(jnp.sum(x[:, :HEAD], axis=-1, keepdims=True), (C, C)),
                 jnp.broadcast_to(jnp.sum(x[:, HEAD:], axis=-1, keepdims=True), (C, C))], axis=1)

        for c in range(G):
            a[c] = jnp.where(lev == DIAG, head_sums(q[c] * k[c]), a[c])
            a[c] = jnp.where(lev == 0, head_sums(q[c] * f[c] * pltpu.roll(k[c], 1, axis=0)), a[c])

        upd0 = [_dot_tn(vb[c][:, :HEAD], kh[c][:, :HEAD]) for c in range(G)]
        upd1 = [_dot_tn(vb[c][:, HEAD:], kh[c][:, HEAD:]) for c in range(G)]
        o_intra = [_dot(a[c].astype(BF16), by_head(vb[c])) for c in range(G)]
        d_s = [jnp.exp2(x) for x in b_last]
        s0 = st_ref[0]
        s1 = st_ref[1]
        o = []
        for c in range(G):
            s_bd = jnp.concatenate(
                [jnp.concatenate([s0.astype(BF16), zs], axis=1),
                 jnp.concatenate([zs, s1.astype(BF16)], axis=1)], axis=0)
            o.append(_dot_nt(qh[c], s_bd) + o_intra[c])
            s0 = s0 * d_s[c][:, :HEAD] + upd0[c]
            s1 = s1 * d_s[c][:, HEAD:] + upd1[c]
        st_ref[0] = s0
        st_ref[1] = s1

        o_all = jnp.concatenate(o, axis=0)
        sq = o_all * o_all
        ms = jnp.concatenate(
            [jnp.broadcast_to(jnp.mean(sq[:, :HEAD], axis=-1, keepdims=True), (R, HEAD)),
             jnp.broadcast_to(jnp.mean(sq[:, HEAD:], axis=-1, keepdims=True), (R, HEAD))], axis=1)
        g = g_ref[rs, :]
        o_ref[rs, :] = (o_all * lax.rsqrt(ms + EPS) * gn * _silu(g)).astype(o_ref.dtype)
        return carry

    lax.fori_loop(0, n_chunks // G, body, 0)


def _hgrn_recurrence(proj, lb_logits, g_norm, batch, seq, slot, side=()):
    M, D4 = proj.shape
    D = D4 // 4
    W = 2 * HEAD
    npair = D // W
    tb = _tile(seq, 2048, CHUNK)
    nt = seq // tb
    group = _tile(tb // CHUNK, 8, 1)
    consts = _hgrn_constants(CHUNK, W)

    def col(part):
        return pl.BlockSpec((tb, W), lambda b, p, t: (b * nt + t, part * npair + p))

    def whole(arr):
        return pl.BlockSpec(arr.shape, lambda b, p, t: (0, 0))

    grid = (batch, npair, nt)
    casts = _side_casts(side, grid)
    if casts is None:
        return (_hgrn_recurrence(proj, lb_logits, g_norm, batch, seq, slot)[0],
                [s.astype(BF16) for s in side])
    flat, side_specs, side_shapes = casts
    outs = pl.pallas_call(
        functools.partial(_hgrn_kernel, n_chunks=tb // CHUNK, slot=slot, group=group,
                          n_side=len(flat)),
        out_shape=[jax.ShapeDtypeStruct((M, D), BF16)] + side_shapes,
        grid=grid,
        in_specs=[col(0), col(1), col(2), col(3),
                  pl.BlockSpec((lb_logits.shape[0], W), lambda b, p, t: (0, p)),
                  pl.BlockSpec((1, W), lambda b, p, t: (0, p))]
        + [whole(c) for c in consts] + side_specs,
        out_specs=[pl.BlockSpec((tb, W), lambda b, p, t: (b * nt + t, p))] + side_specs,
        scratch_shapes=[pltpu.VMEM((2, HEAD, HEAD), F32)],
        compiler_params=_params(("parallel", "parallel", "arbitrary")),
        name="hgrn_recurrence",
    )(proj, proj, proj, proj, lb_logits, g_norm.reshape(1, D), *consts, *flat)
    return outs[0], [o.reshape(s.shape) for o, s in zip(outs[1:], side)]


def kernel(x, c, w_cond, b_cond, ada_w, ada_b, lb_logits, hgrn_w_in, hgrn_norm, hgrn_w_out,
           sconv_w_in, sconv_conv_w, sconv_w_out, ffn_w_up, ffn_conv_w, ffn_conv_b,
           ffn_w_down, final_norm):
    B, T, D = x.shape
    depth = ada_w.shape[0]
    F = ffn_conv_b.shape[1]
    M = B * T
    n_mixers = 2

    c_emb = _cond_matmul(c, w_cond[None], b_cond[None], act=True)[0]
    mod_i = _cond_matmul(c_emb, ada_w, ada_b, act=False, layers=1)[0]

    w_hin = hgrn_w_in.astype(BF16)

    x2 = x.reshape(M, D)
    for i in range(depth):
        mod3 = mod_i.reshape(B, 1, N_MOD * D)
        j = i // n_mixers
        hm = _prenorm(x2.reshape(B, T, D), mod3, 0, 1).reshape(M, D)
        if i % n_mixers == 0:
            first = i == 0
            proj, copies = _matmul(hm, w_hin, j, F32, side=(hgrn_w_out, ffn_w_up) if first else ())
            if first:
                w_hout, w_up = copies
            y, copies = _hgrn_recurrence(proj, lb_logits, hgrn_norm[j], B, T, slot=i,
                                         side=(sconv_w_in, sconv_w_out, ffn_w_down) if first else ())
            if first:
                w_sin, w_sout, w_down = copies
            x2, hf = _out_proj_norm(y, w_hout, j, x2, mod3, 2, 3, 4, T, bm=OUT_BM)
        else:
            y = _sconv_in(hm, w_sin, j, sconv_conv_w[j].T, T)
            x2, hf = _out_proj_norm(y, w_sout, j, x2, mod3, 2, 3, 4, T, bm=OUT_BM)

        if i + 1 < depth:
            h, mod_i = _ffn_up(hf, w_up, i, ffn_conv_w[i].T, ffn_conv_b[i].reshape(1, F), T,
                               mod_job=(c_emb, ada_w, ada_b, i + 1))
        else:
            h = _ffn_up(hf, w_up, i, ffn_conv_w[i].T, ffn_conv_b[i].reshape(1, F), T)
        x2 = _matmul_residual(h, w_down, i, x2, mod3, 5, T, blk=512)
    return _final_norm(x2, final_norm).reshape(B, T, D)
```

```python
import functools

import numpy as np
import jax
import jax.numpy as jnp
from jax import lax
from jax.experimental import pallas as pl
from jax.experimental.pallas import tpu as pltpu

EPS = 1e-6
CHUNK = 64
HEAD = 128
N_MOD = 6
LOG2E = 1.4426950408889634
F32 = jnp.float32
BF16 = jnp.bfloat16

V7X_VMEM_BYTES = 64 * 1024 * 1024
VMEM_LIMIT = V7X_VMEM_BYTES - 8 * 1024 * 1024
LANE = 128
SUBLANE = 8
BF16_SUBLANE = 16
FFN_BN = 512
OUT_BM = 256
OUT_SLABS = 8


def _tile(n, pref, align):
    t = min(pref, n)
    t -= t % align
    while t >= align:
        if n % t == 0:
            return t
        t -= align
    return n


def _params(sem, vmem=VMEM_LIMIT):
    return pltpu.CompilerParams(dimension_semantics=sem, vmem_limit_bytes=vmem)


def _dot(a, b):
    return jnp.dot(a, b, preferred_element_type=F32)


def _dot_nt(a, b):
    return lax.dot_general(a, b, (((1,), (1,)), ((), ())), preferred_element_type=F32)


def _dot_tn(a, b):
    return lax.dot_general(a, b, (((0,), (0,)), ((), ())), preferred_element_type=F32)


def _silu(x):
    return x * jax.nn.sigmoid(x)


def _cond_kernel(c_ref, w_ref, b_ref, o_ref, *, act):
    acc = _dot(c_ref[...].astype(BF16), w_ref[...].astype(BF16)) + b_ref[...]
    o_ref[...] = _silu(acc) if act else acc


def _cond_matmul(c, w, b, *, act, first=0, layers=None):
    _, K, N = w.shape
    L = w.shape[0] - first if layers is None else layers
    B = c.shape[0]
    bn = _tile(N, 512, LANE)
    return pl.pallas_call(
        functools.partial(_cond_kernel, act=act),
        out_shape=jax.ShapeDtypeStruct((L, B, N), F32),
        grid=(L, N // bn),
        in_specs=[
            pl.BlockSpec((B, K), lambda l, j: (0, 0)),
            pl.BlockSpec((None, K, bn), lambda l, j: (first + l, 0, j)),
            pl.BlockSpec((None, 1, bn), lambda l, j: (first + l, 0, j)),
        ],
        out_specs=pl.BlockSpec((None, B, bn), lambda l, j: (l, 0, j)),
        compiler_params=_params(("parallel", "parallel")),
        name="cond_matmul",
    )(c, w, b.reshape(b.shape[0], 1, N))


def _prenorm_kernel(x_ref, sh_ref, sc_ref, o_ref):
    x = x_ref[...]
    y = x * lax.rsqrt(jnp.mean(x * x, axis=-1, keepdims=True) + EPS)
    o_ref[...] = (y * (1.0 + sc_ref[...]) + sh_ref[...]).astype(o_ref.dtype)


def _prenorm(x3, mod3, k_shift, k_scale):
    B, T, D = x3.shape
    bt = _tile(T, 512, SUBLANE)
    return pl.pallas_call(
        _prenorm_kernel,
        out_shape=jax.ShapeDtypeStruct((B, T, D), BF16),
        grid=(B, T // bt),
        in_specs=[
            pl.BlockSpec((None, bt, D), lambda b, t: (b, t, 0)),
            pl.BlockSpec((None, 1, D), lambda b, t: (b, 0, k_shift)),
            pl.BlockSpec((None, 1, D), lambda b, t: (b, 0, k_scale)),
        ],
        out_specs=pl.BlockSpec((None, bt, D), lambda b, t: (b, t, 0)),
        compiler_params=_params(("parallel", "parallel")),
        name="prenorm",
    )(x3, mod3, mod3)


def _final_norm_kernel(x_ref, w_ref, o_ref):
    x = x_ref[...]
    o_ref[...] = x * lax.rsqrt(jnp.mean(x * x, axis=-1, keepdims=True) + EPS) * w_ref[...]


def _final_norm(x2, w):
    M, D = x2.shape
    bt = _tile(M, 512, SUBLANE)
    return pl.pallas_call(
        _final_norm_kernel,
        out_shape=jax.ShapeDtypeStruct((M, D), F32),
        grid=(M // bt,),
        in_specs=[pl.BlockSpec((bt, D), lambda i: (i, 0)),
                  pl.BlockSpec((1, D), lambda i: (0, 0))],
        out_specs=pl.BlockSpec((bt, D), lambda i: (i, 0)),
        compiler_params=_params(("parallel",)),
        name="final_norm",
    )(x2, w.reshape(1, D))


def _cast_split(shape, steps):
    rows, cols = shape
    nc = 1
    while nc <= steps:
        nr, rem = divmod(steps, nc)
        if (rem == 0 and rows % nr == 0 and cols % nc == 0
                and (rows // nr) % BF16_SUBLANE == 0 and (cols // nc) % LANE == 0):
            return (rows // nr, cols // nc), nc
        nc *= 2
    return None


def _side_casts(side, grid):
    flat = [s.reshape(-1, s.shape[-1]) for s in side]
    splits = [_cast_split(f.shape, int(np.prod(grid))) for f in flat]
    if any(sp is None for sp in splits):
        return None

    def spec(blk, nc):
        def index(*idx):
            step = functools.reduce(lambda acc, t: acc * t[1] + t[0], zip(idx, grid), 0)
            return step // nc, step % nc
        return pl.BlockSpec(blk, index)

    return (flat, [spec(*sp) for sp in splits],
            [jax.ShapeDtypeStruct(f.shape, BF16) for f in flat])


def _cast_blocks(srcs, dsts):
    for src, dst in zip(srcs, dsts):
        dst[...] = src[...].astype(dst.dtype)


def _mm_cast_kernel(a_ref, w_ref, *refs, n_side):
    o_ref = refs[n_side]
    o_ref[...] = _dot(a_ref[...], w_ref[...]).astype(o_ref.dtype)
    _cast_blocks(refs[:n_side], refs[n_side + 1:])


def _matmul(a, w, layer, out_dtype, side=()):
    M, K = a.shape
    N = w.shape[2]
    bm = _tile(M, 1024, SUBLANE)
    bn = _tile(N, 1024, LANE)
    grid = (N // bn, M // bm)
    casts = _side_casts(side, grid)
    if casts is None:
        return _matmul(a, w, layer, out_dtype)[0], [s.astype(BF16) for s in side]
    flat, side_specs, side_shapes = casts
    outs = pl.pallas_call(
        functools.partial(_mm_cast_kernel, n_side=len(flat)),
        out_shape=[jax.ShapeDtypeStruct((M, N), out_dtype)] + side_shapes,
        grid=grid,
        in_specs=[pl.BlockSpec((bm, K), lambda j, i: (i, 0)),
                  pl.BlockSpec((None, K, bn), lambda j, i: (layer, 0, j))] + side_specs,
        out_specs=[pl.BlockSpec((bm, bn), lambda j, i: (i, j))] + side_specs,
        compiler_params=_params(("parallel", "parallel"), V7X_VMEM_BYTES - 4 * 1024 * 1024),
        name="matmul",
    )(a, w, *flat)
    return outs[0], [o.reshape(s.shape) for o, s in zip(outs[1:], side)]


def _mm_res_kernel(a_ref, w_ref, x_ref, g_ref, o_ref):
    o_ref[...] = x_ref[...] + g_ref[...] * _dot(a_ref[...], w_ref[...])


def _matmul_residual(a, w, layer, x2, mod3, k_gate, seq, *, blk):
    M, K = a.shape
    N = w.shape[2]
    bm = _tile(seq, blk, SUBLANE)
    bn = _tile(N, blk, LANE)
    nb = N // bn
    per_seq = seq // bm
    return pl.pallas_call(
        _mm_res_kernel,
        out_shape=jax.ShapeDtypeStruct((M, N), F32),
        grid=(nb, M // bm),
        in_specs=[
            pl.BlockSpec((bm, K), lambda j, i: (i, 0)),
            pl.BlockSpec((None, K, bn), lambda j, i: (layer, 0, j)),
            pl.BlockSpec((bm, bn), lambda j, i: (i, j)),
            pl.BlockSpec((None, 1, bn), lambda j, i: (i // per_seq, 0, k_gate * nb + j)),
        ],
        out_specs=pl.BlockSpec((bm, bn), lambda j, i: (i, j)),
        compiler_params=_params(("parallel", "parallel")),
        name="matmul_residual",
    )(a, w, x2, mod3)


def _out_norm_kernel(a_ref, w_ref, x_ref, g_ref, sh_ref, sc_ref, xo_ref, ho_ref, xs_ref, rs_ref):
    @pl.when(pl.program_id(0) == 0)
    def _():
        xs_ref[...] = jnp.zeros_like(xs_ref)
        rs_ref[...] = jnp.zeros_like(rs_ref)

    ho_ref[...] = (xs_ref[...] * rs_ref[...] * (1.0 + sc_ref[...]) + sh_ref[...]).astype(ho_ref.dtype)
    a = a_ref[...]
    n = xo_ref.shape[1]
    w = n // OUT_SLABS
    ss = None
    for k in range(OUT_SLABS):
        cs = slice(k * w, (k + 1) * w)
        xn = x_ref[:, cs] + g_ref[:, cs] * _dot(a, w_ref[:, cs])
        xo_ref[:, cs] = xn
        xs_ref[:, cs] = xn
        part = jnp.sum(xn * xn, axis=-1, keepdims=True)
        ss = part if ss is None else ss + part
    rs_ref[...] = lax.rsqrt(ss * (1.0 / n) + EPS)


def _out_proj_norm(a, w, layer, x2, mod3, k_gate, k_shift, k_scale, seq, *, bm):
    M, K = a.shape
    N = w.shape[2]
    per_seq = seq // bm
    n = M // bm

    def cur(i):
        return jnp.minimum(i, n - 1)

    def prev(i):
        return jnp.maximum(i - 1, 0)

    return pl.pallas_call(
        _out_norm_kernel,
        out_shape=(jax.ShapeDtypeStruct((M, N), F32), jax.ShapeDtypeStruct((M, N), BF16)),
        grid=(n + 1,),
        in_specs=[
            pl.BlockSpec((bm, K), lambda i: (cur(i), 0)),
            pl.BlockSpec((None, K, N), lambda i: (layer, 0, 0), pipeline_mode=pl.Buffered(1)),
            pl.BlockSpec((bm, N), lambda i: (cur(i), 0)),
            pl.BlockSpec((None, 1, N), lambda i: (cur(i) // per_seq, 0, k_gate)),
            pl.BlockSpec((None, 1, N), lambda i: (prev(i) // per_seq, 0, k_shift)),
            pl.BlockSpec((None, 1, N), lambda i: (prev(i) // per_seq, 0, k_scale)),
        ],
        out_specs=(pl.BlockSpec((bm, N), lambda i: (cur(i), 0)), pl.BlockSpec((bm, N), lambda i: (prev(i), 0))),
        scratch_shapes=[pltpu.VMEM((bm, N), F32), pltpu.VMEM((bm, 1), F32)],
        compiler_params=_params(("arbitrary",), V7X_VMEM_BYTES - 1024 * 1024),
        name="out_proj_norm",
    )(a, w, x2, mod3, mod3, mod3)


def _causal_conv3(p, halo, w):
    row = lax.broadcasted_iota(jnp.int32, p.shape, 0)
    s1 = pltpu.roll(p, 1, axis=0)
    s2 = pltpu.roll(p, 2, axis=0)
    h7 = halo[SUBLANE - 1:SUBLANE, :]
    h6 = halo[SUBLANE - 2:SUBLANE - 1, :]
    s1 = jnp.where(row == 0, h7, s1)
    s2 = jnp.where(row == 0, h6, jnp.where(row == 1, h7, s2))
    return w[0:1, :] * s2 + w[1:2, :] * s1 + w[2:3, :] * p


def _sconv_in_kernel(a_ref, wb_ref, wc_ref, wx_ref, cw_ref, o_ref, halo_ref, *, per_seq):
    i = pl.program_id(1)
    a = a_ref[...]
    p = _dot(a, wc_ref[...]) * _dot(a, wx_ref[...])
    halo = halo_ref[...]
    halo = jnp.where(i % per_seq == 0, jnp.zeros_like(halo), halo)
    halo_ref[...] = p[p.shape[0] - SUBLANE:, :]
    o_ref[...] = (_dot(a, wb_ref[...]) * _causal_conv3(p, halo, cw_ref[...])).astype(o_ref.dtype)


def _sconv_in(a, w_in, layer, conv_w3, seq):
    M, D = a.shape
    bm = _tile(seq, 1024, SUBLANE)
    bn = _tile(D, 512, LANE)
    nb = D // bn
    return pl.pallas_call(
        functools.partial(_sconv_in_kernel, per_seq=seq // bm),
        out_shape=jax.ShapeDtypeStruct((M, D), BF16),
        grid=(nb, M // bm),
        in_specs=[
            pl.BlockSpec((bm, D), lambda j, i: (i, 0)),
            pl.BlockSpec((None, D, bn), lambda j, i: (layer, 0, j)),
            pl.BlockSpec((None, D, bn), lambda j, i: (layer, 0, nb + j)),
            pl.BlockSpec((None, D, bn), lambda j, i: (layer, 0, 2 * nb + j)),
            pl.BlockSpec((3, bn), lambda j, i: (0, j)),
        ],
        out_specs=pl.BlockSpec((bm, bn), lambda j, i: (i, j)),
        scratch_shapes=[pltpu.VMEM((SUBLANE, bn), F32)],
        compiler_params=_params(("arbitrary", "arbitrary")),
        name="sconv_in",
    )(a, w_in, w_in, w_in, conv_w3)


def _ffn_up_kernel(a_ref, wa_ref, wu0_ref, wu1_ref, cw_ref, cb_ref, *refs, per_seq, ragged):
    if len(refs) == 2:
        (o_ref, halo_ref), job = refs, None
    else:
        ce_ref, aw_ref, ab_ref, o_ref, mo_ref, halo_ref = refs
        job = True
    j = pl.program_id(0)
    i = pl.program_id(1)
    bn = o_ref.shape[1]
    half = bn // 2

    def compute(width):
        if job:
            mo_ref[...] = _dot(ce_ref[...].astype(BF16), aw_ref[...].astype(BF16)) + ab_ref[...]
        a = a_ref[...]
        p = _dot(a, wa_ref[:, :width])
        halo = halo_ref[:, :width]
        halo = jnp.where(i % per_seq == 0, jnp.zeros_like(halo), halo)
        halo_ref[:, :width] = p[p.shape[0] - SUBLANE:, :]
        gate = _silu(_causal_conv3(p, halo, cw_ref[:, :width]) + cb_ref[:, :width])
        u = _dot(a, wu0_ref[...])
        if width == bn:
            u = jnp.concatenate([u, _dot(a, wu1_ref[...])], axis=1)
        o_ref[:, :width] = (gate * u).astype(o_ref.dtype)

    if ragged:
        last = pl.num_programs(0) - 1
        pl.when(j < last)(lambda: compute(bn))
        pl.when(j == last)(lambda: compute(half))
    else:
        compute(bn)


def _ffn_up(a, w_up, layer, conv_w3, conv_b, seq, mod_job=None):
    M, D = a.shape
    F = w_up.shape[2] // 2
    bm = _tile(seq, 1024, SUBLANE)
    bn = FFN_BN
    half = bn // 2
    assert F % half == 0
    nb = pl.cdiv(F, bn)
    u0 = F // half
    last = 2 * F // half - 1
    ni = M // bm
    in_specs = [
        pl.BlockSpec((bm, D), lambda j, i: (i, 0)),
        pl.BlockSpec((None, D, bn), lambda j, i: (layer, 0, j)),
        pl.BlockSpec((None, D, half), lambda j, i: (layer, 0, u0 + 2 * j)),
        pl.BlockSpec((None, D, half), lambda j, i: (layer, 0, jnp.minimum(u0 + 2 * j + 1, last))),
        pl.BlockSpec((3, bn), lambda j, i: (0, j)),
        pl.BlockSpec((1, bn), lambda j, i: (0, j)),
    ]
    out_shape = [jax.ShapeDtypeStruct((M, F), BF16)]
    out_specs = [pl.BlockSpec((bm, bn), lambda j, i: (i, j))]
    args = [a, w_up, w_up, w_up, conv_w3, conv_b]
    if mod_job is not None:
        c_emb, ada_w, ada_b, nxt = mod_job
        B, n_mod = c_emb.shape[0], ada_w.shape[2]
        n_tiles = n_mod // LANE
        if nb * ni < n_tiles:
            return (_ffn_up(a, w_up, layer, conv_w3, conv_b, seq),
                    _cond_matmul(c_emb, ada_w, ada_b, act=False, first=nxt, layers=1)[0])

        def tile_of(j, i):
            return jnp.minimum(j * ni + i, n_tiles - 1)

        in_specs += [pl.BlockSpec(c_emb.shape, lambda j, i: (0, 0)),
                     pl.BlockSpec((None, D, LANE), lambda j, i: (nxt, 0, tile_of(j, i))),
                     pl.BlockSpec((None, 1, LANE), lambda j, i: (nxt, 0, tile_of(j, i)))]
        out_shape.append(jax.ShapeDtypeStruct((B, n_mod), F32))
        out_specs.append(pl.BlockSpec((B, LANE), lambda j, i: (0, tile_of(j, i))))
        args += [c_emb, ada_w, ada_b.reshape(ada_b.shape[0], 1, n_mod)]
    outs = pl.pallas_call(
        functools.partial(_ffn_up_kernel, per_seq=seq // bm, ragged=F % bn != 0),
        out_shape=out_shape,
        grid=(nb, ni),
        in_specs=in_specs,
        out_specs=out_specs,
        scratch_shapes=[pltpu.VMEM((SUBLANE, bn), F32)],
        compiler_params=_params(("arbitrary", "arbitrary")),
        name="ffn_up",
    )(*args)
    return outs if mod_job is not None else outs[0]


DIAG = -2
N_FINE = 3
LOG_SPLIT = 2


def _hgrn_constants(C, W):
    u = np.arange(C)[None, :]
    t = np.arange(C)[:, None]
    bands = [u <= t]
    masks = []
    for l in range(1, N_FINE):
        h = 1 << l
        m = (t // (2 * h)) * (2 * h) + h
        upper = t >= m
        bands.append(np.where(upper, (u > m) & (u <= t), (u > t) & (u <= m)))
        masks.append(np.broadcast_to(upper[:SUBLANE], (SUBLANE, W)))
    d = np.concatenate(bands, axis=0).astype(np.float32)
    dmat = np.concatenate([d] * LOG_SPLIT, axis=1)
    rowmask = np.concatenate(masks, axis=0).astype(np.float32)
    s = np.arange(2 * C)[None, :] % C
    x = t ^ s
    lev = np.where(t > s, np.floor(np.log2(np.maximum(x, 1))), np.where(t == s, DIAG, -1))
    return jnp.asarray(dmat, BF16), jnp.asarray(rowmask, F32), jnp.asarray(lev, jnp.int32)


def _hgrn_kernel(q_ref, f_ref, v_ref, g_ref, lbl_ref, gn_ref, dmat_ref, rmask_ref, lev_ref,
                 *refs, n_chunks, slot, group, n_side):
    o_ref, st_ref = refs[n_side], refs[-1]
    _cast_blocks(refs[:n_side], refs[n_side + 1:-1])
    C = CHUNK
    W = 2 * HEAD
    n_lvl = int(np.log2(C))

    @pl.when(pl.program_id(2) == 0)
    def _():
        st_ref[...] = jnp.zeros_like(st_ref)

    rows = [lbl_ref[k:k + 1, :] for k in range(lbl_ref.shape[0])]
    mx = functools.reduce(jnp.maximum, rows)
    ex = [jnp.exp(r - mx) for r in rows]
    lb = functools.reduce(jnp.add, ex[:slot + 1]) / functools.reduce(jnp.add, ex)
    gn = gn_ref[...]
    zc = jnp.zeros((C, HEAD), BF16)
    zs = jnp.zeros((HEAD, HEAD), BF16)

    def by_head(x):
        return jnp.concatenate(
            [jnp.concatenate([x[:, :HEAD], zc], axis=1),
             jnp.concatenate([zc, x[:, HEAD:]], axis=1)], axis=0)

    G = group
    R = G * C

    def per_chunk(x):
        return [x[c * C:(c + 1) * C] for c in range(G)]

    def body(gi, carry):
        rs = pl.ds(pl.multiple_of(gi * R, R), R)
        q_all = q_ref[rs, :]
        f_all = lb + (1.0 - lb) * jax.nn.sigmoid(f_ref[rs, :])
        k_all = 1.0 - f_all
        lg = jnp.log(f_all) * LOG2E
        pieces, rem = [], lg
        for _ in range(LOG_SPLIT - 1):
            pieces.append(rem.astype(BF16))
            rem = rem - pieces[-1].astype(F32)
        pieces.append(rem.astype(BF16))
        q, k, f = per_chunk(q_all), per_chunk(k_all), per_chunk(f_all)
        vb = per_chunk(v_ref[rs, :].astype(BF16))
        dmat = dmat_ref[...]
        args = [_dot(dmat, jnp.concatenate(list(xs), axis=0))
                for xs in zip(*[per_chunk(x) for x in pieces])]
        b = [x[0:C] for x in args]
        b_last = [x[C - 1:C, :] for x in b]
        e_fine = [jnp.exp2(x[C:]) for x in args]
        lev = lev_ref[...]

        a = [0.0] * G
        for l in range(1, N_FINE):
            up = jnp.tile(rmask_ref[(l - 1) * SUBLANE:l * SUBLANE, :], (C // SUBLANE, 1)) != 0.0
            for c in range(G):
                z = (jnp.where(up, q[c], k[c]) * e_fine[c][(l - 1) * C:l * C]).astype(BF16)
                a[c] = jnp.where(lev == l, _dot_nt(z, by_head(z)), a[c])
        lo = 1 << N_FINE
        qh = [[q[c][:lo] * jnp.exp2(b[c][:lo])] for c in range(G)]
        kh = [[k[c][C - lo:] * jnp.exp2(b_last[c] - b[c][C - lo:])] for c in range(G)]
        for l in range(N_FINE, n_lvl):
            h = 1 << l
            zh = jnp.zeros((h, W), BF16)
            for c in range(G):
                qe, ke = [], []
                for base in range(0, C, 2 * h):
                    m = base + h
                    bm = b[c][m:m + 1, :]
                    qp = q[c][m:m + h] * jnp.exp2(b[c][m:m + h] - bm)
                    kp = k[c][base:m] * jnp.exp2(bm - b[c][base:m])
                    if base == 0:
                        qh[c].append(qp * jnp.exp2(bm))
                    if base == C - 2 * h:
                        kh[c].insert(0, kp * jnp.exp2(b_last[c] - bm))
                    qe += [zh, qp.astype(BF16)]
                    ke += [kp.astype(BF16), zh]
                p = _dot_nt(jnp.concatenate(qe, axis=0), by_head(jnp.concatenate(ke, axis=0)))
                a[c] = jnp.where(lev == l, p, a[c])
        qh = [jnp.concatenate(x, axis=0).astype(BF16) for x in qh]
        kh = [jnp.concatenate(x, axis=0).astype(BF16) for x in kh]

        def head_sums(x):
            return jnp.concatenate(
                [jnp.broadcast_to(jnp.sum(x[:, :HEAD], axis=-1, keepdims=True), (C, C)),
                 jnp.broadcast_to(jnp.sum(x[:, HEAD:], axis=-1, keepdims=True), (C, C))], axis=1)

        for c in range(G):
            a[c] = jnp.where(lev == DIAG, head_sums(q[c] * k[c]), a[c])
            a[c] = jnp.where(lev == 0, head_sums(q[c] * f[c] * pltpu.roll(k[c], 1, axis=0)), a[c])

        upd0 = [_dot_tn(vb[c][:, :HEAD], kh[c][:, :HEAD]) for c in range(G)]
        upd1 = [_dot_tn(vb[c][:, HEAD:], kh[c][:, HEAD:]) for c in range(G)]
        o_intra = [_dot(a[c].astype(BF16), by_head(vb[c])) for c in range(G)]
        d_s = [jnp.exp2(x) for x in b_last]
        s0 = st_ref[0]
        s1 = st_ref[1]
        o = []
        for c in range(G):
            s_bd = jnp.concatenate(
                [jnp.concatenate([s0.astype(BF16), zs], axis=1),
                 jnp.concatenate([zs, s1.astype(BF16)], axis=1)], axis=0)
            o.append(_dot_nt(qh[c], s_bd) + o_intra[c])
            s0 = s0 * d_s[c][:, :HEAD] + upd0[c]
            s1 = s1 * d_s[c][:, HEAD:] + upd1[c]
        st_ref[0] = s0
        st_ref[1] = s1

        o_all = jnp.concatenate(o, axis=0)
        sq = o_all * o_all
        ms = jnp.concatenate(
            [jnp.broadcast_to(jnp.mean(sq[:, :HEAD], axis=-1, keepdims=True), (R, HEAD)),
             jnp.broadcast_to(jnp.mean(sq[:, HEAD:], axis=-1, keepdims=True), (R, HEAD))], axis=1)
        g = g_ref[rs, :]
        o_ref[rs, :] = (o_all * lax.rsqrt(ms + EPS) * gn * _silu(g)).astype(o_ref.dtype)
        return carry

    lax.fori_loop(0, n_chunks // G, body, 0)


def _hgrn_recurrence(proj, lb_logits, g_norm, batch, seq, slot, side=()):
    M, D4 = proj.shape
    D = D4 // 4
    W = 2 * HEAD
    npair = D // W
    tb = _tile(seq, 2048, CHUNK)
    nt = seq // tb
    group = _tile(tb // CHUNK, 8, 1)
    consts = _hgrn_constants(CHUNK, W)

    def col(part):
        return pl.BlockSpec((tb, W), lambda b, p, t: (b * nt + t, part * npair + p))

    def whole(arr):
        return pl.BlockSpec(arr.shape, lambda b, p, t: (0, 0))

    grid = (batch, npair, nt)
    casts = _side_casts(side, grid)
    if casts is None:
        return (_hgrn_recurrence(proj, lb_logits, g_norm, batch, seq, slot)[0],
                [s.astype(BF16) for s in side])
    flat, side_specs, side_shapes = casts
    outs = pl.pallas_call(
        functools.partial(_hgrn_kernel, n_chunks=tb // CHUNK, slot=slot, group=group,
                          n_side=len(flat)),
        out_shape=[jax.ShapeDtypeStruct((M, D), BF16)] + side_shapes,
        grid=grid,
        in_specs=[col(0), col(1), col(2), col(3),
                  pl.BlockSpec((lb_logits.shape[0], W), lambda b, p, t: (0, p)),
                  pl.BlockSpec((1, W), lambda b, p, t: (0, p))]
        + [whole(c) for c in consts] + side_specs,
        out_specs=[pl.BlockSpec((tb, W), lambda b, p, t: (b * nt + t, p))] + side_specs,
        scratch_shapes=[pltpu.VMEM((2, HEAD, HEAD), F32)],
        compiler_params=_params(("parallel", "parallel", "arbitrary")),
        name="hgrn_recurrence",
    )(proj, proj, proj, proj, lb_logits, g_norm.reshape(1, D), *consts, *flat)
    return outs[0], [o.reshape(s.shape) for o, s in zip(outs[1:], side)]


def kernel(x, c, w_cond, b_cond, ada_w, ada_b, lb_logits, hgrn_w_in, hgrn_norm, hgrn_w_out,
           sconv_w_in, sconv_conv_w, sconv_w_out, ffn_w_up, ffn_conv_w, ffn_conv_b,
           ffn_w_down, final_norm):
    B, T, D = x.shape
    depth = ada_w.shape[0]
    F = ffn_conv_b.shape[1]
    M = B * T
    n_mixers = 2

    c_emb = _cond_matmul(c, w_cond[None], b_cond[None], act=True)[0]
    mod_i = _cond_matmul(c_emb, ada_w, ada_b, act=False, layers=1)[0]

    w_hin = hgrn_w_in.astype(BF16)

    x2 = x.reshape(M, D)
    for i in range(depth):
        mod3 = mod_i.reshape(B, 1, N_MOD * D)
        j = i // n_mixers
        hm = _prenorm(x2.reshape(B, T, D), mod3, 0, 1).reshape(M, D)
        if i % n_mixers == 0:
            first = i == 0
            proj, copies = _matmul(hm, w_hin, j, F32, side=(hgrn_w_out, ffn_w_up) if first else ())
            if first:
                w_hout, w_up = copies
            y, copies = _hgrn_recurrence(proj, lb_logits, hgrn_norm[j], B, T, slot=i,
                                         side=(sconv_w_in, sconv_w_out, ffn_w_down) if first else ())
            if first:
                w_sin, w_sout, w_down = copies
            x2, hf = _out_proj_norm(y, w_hout, j, x2, mod3, 2, 3, 4, T, bm=OUT_BM)
        else:
            y = _sconv_in(hm, w_sin, j, sconv_conv_w[j].T, T)
            x2, hf = _out_proj_norm(y, w_sout, j, x2, mod3, 2, 3, 4, T, bm=OUT_BM)

        if i + 1 < depth:
            h, mod_i = _ffn_up(hf, w_up, i, ffn_conv_w[i].T, ffn_conv_b[i].reshape(1, F), T,
                               mod_job=(c_emb, ada_w, ada_b, i + 1))
        else:
            h = _ffn_up(hf, w_up, i, ffn_conv_w[i].T, ffn_conv_b[i].reshape(1, F), T)
        x2 = _matmul_residual(h, w_down, i, x2, mod3, 5, T, blk=512)
    return _final_norm(x2, final_norm).reshape(B, T, D)
```

```python
import functools

import numpy as np
import jax
import jax.numpy as jnp
from jax import lax
from jax.experimental import pallas as pl
from jax.experimental.pallas import tpu as pltpu

EPS = 1e-6
CHUNK = 64
HEAD = 128
N_MOD = 6
LOG2E = 1.4426950408889634
F32 = jnp.float32
BF16 = jnp.bfloat16

V7X_VMEM_BYTES = 64 * 1024 * 1024
VMEM_LIMIT = V7X_VMEM_BYTES - 8 * 1024 * 1024
LANE = 128
SUBLANE = 8
BF16_SUBLANE = 16
FFN_BN = 512
OUT_BM = 256
OUT_SLABS = 8


def _tile(n, pref, align):
    t = min(pref, n)
    t -= t % align
    while t >= align:
        if n % t == 0:
            return t
        t -= align
    return n


def _params(sem, vmem=VMEM_LIMIT):
    return pltpu.CompilerParams(dimension_semantics=sem, vmem_limit_bytes=vmem)


def _dot(a, b):
    return jnp.dot(a, b, preferred_element_type=F32)


def _dot_nt(a, b):
    return lax.dot_general(a, b, (((1,), (1,)), ((), ())), preferred_element_type=F32)


def _dot_tn(a, b):
    return lax.dot_general(a, b, (((0,), (0,)), ((), ())), preferred_element_type=F32)


def _silu(x):
    return x * jax.nn.sigmoid(x)


def _sigmoid_tanh(x):
    return 0.5 + 0.5 * jnp.tanh(0.5 * x)


def _cond_kernel(c_ref, w_ref, b_ref, o_ref, *, act):
    acc = _dot(c_ref[...].astype(BF16), w_ref[...].astype(BF16)) + b_ref[...]
    o_ref[...] = _silu(acc) if act else acc


def _cond_matmul(c, w, b, *, act, first=0, layers=None):
    _, K, N = w.shape
    L = w.shape[0] - first if layers is None else layers
    B = c.shape[0]
    bn = _tile(N, 512, LANE)
    return pl.pallas_call(
        functools.partial(_cond_kernel, act=act),
        out_shape=jax.ShapeDtypeStruct((L, B, N), F32),
        grid=(L, N // bn),
        in_specs=[
            pl.BlockSpec((B, K), lambda l, j: (0, 0)),
            pl.BlockSpec((None, K, bn), lambda l, j: (first + l, 0, j)),
            pl.BlockSpec((None, 1, bn), lambda l, j: (first + l, 0, j)),
        ],
        out_specs=pl.BlockSpec((None, B, bn), lambda l, j: (l, 0, j)),
        compiler_params=_params(("parallel", "parallel")),
        name="cond_matmul",
    )(c, w, b.reshape(b.shape[0], 1, N))


def _prenorm_kernel(x_ref, sh_ref, sc_ref, o_ref):
    x = x_ref[...]
    y = x * lax.rsqrt(jnp.mean(x * x, axis=-1, keepdims=True) + EPS)
    o_ref[...] = (y * (1.0 + sc_ref[...]) + sh_ref[...]).astype(o_ref.dtype)


def _prenorm(x3, mod3, k_shift, k_scale):
    B, T, D = x3.shape
    bt = _tile(T, 512, SUBLANE)
    return pl.pallas_call(
        _prenorm_kernel,
        out_shape=jax.ShapeDtypeStruct((B, T, D), BF16),
        grid=(B, T // bt),
        in_specs=[
            pl.BlockSpec((None, bt, D), lambda b, t: (b, t, 0)),
            pl.BlockSpec((None, 1, D), lambda b, t: (b, 0, k_shift)),
            pl.BlockSpec((None, 1, D), lambda b, t: (b, 0, k_scale)),
        ],
        out_specs=pl.BlockSpec((None, bt, D), lambda b, t: (b, t, 0)),
        compiler_params=_params(("parallel", "parallel")),
        name="prenorm",
    )(x3, mod3, mod3)


def _final_norm_kernel(x_ref, w_ref, o_ref):
    x = x_ref[...]
    o_ref[...] = x * lax.rsqrt(jnp.mean(x * x, axis=-1, keepdims=True) + EPS) * w_ref[...]


def _final_norm(x2, w):
    M, D = x2.shape
    bt = _tile(M, 512, SUBLANE)
    return pl.pallas_call(
        _final_norm_kernel,
        out_shape=jax.ShapeDtypeStruct((M, D), F32),
        grid=(M // bt,),
        in_specs=[pl.BlockSpec((bt, D), lambda i: (i, 0)),
                  pl.BlockSpec((1, D), lambda i: (0, 0))],
        out_specs=pl.BlockSpec((bt, D), lambda i: (i, 0)),
        compiler_params=_params(("parallel",)),
        name="final_norm",
    )(x2, w.reshape(1, D))


def _cast_split(shape, steps):
    rows, cols = shape
    nc = 1
    while nc <= steps:
        nr, rem = divmod(steps, nc)
        if (rem == 0 and rows % nr == 0 and cols % nc == 0
                and (rows // nr) % BF16_SUBLANE == 0 and (cols // nc) % LANE == 0):
            return (rows // nr, cols // nc), nc
        nc *= 2
    return None


def _side_casts(side, grid):
    flat = [s.reshape(-1, s.shape[-1]) for s in side]
    splits = [_cast_split(f.shape, int(np.prod(grid))) for f in flat]
    if any(sp is None for sp in splits):
        return None

    def spec(blk, nc):
        def index(*idx):
            step = functools.reduce(lambda acc, t: acc * t[1] + t[0], zip(idx, grid), 0)
            return step // nc, step % nc
        return pl.BlockSpec(blk, index)

    return (flat, [spec(*sp) for sp in splits],
            [jax.ShapeDtypeStruct(f.shape, BF16) for f in flat])


def _cast_blocks(srcs, dsts):
    for src, dst in zip(srcs, dsts):
        dst[...] = src[...].astype(dst.dtype)


def _mm_cast_kernel(a_ref, w_ref, *refs, n_side):
    o_ref = refs[n_side]
    o_ref[...] = _dot(a_ref[...], w_ref[...]).astype(o_ref.dtype)
    _cast_blocks(refs[:n_side], refs[n_side + 1:])


def _matmul(a, w, layer, out_dtype, side=()):
    M, K = a.shape
    N = w.shape[2]
    bm = _tile(M, 1024, SUBLANE)
    bn = _tile(N, 1024, LANE)
    grid = (N // bn, M // bm)
    casts = _side_casts(side, grid)
    if casts is None:
        return _matmul(a, w, layer, out_dtype)[0], [s.astype(BF16) for s in side]
    flat, side_specs, side_shapes = casts
    outs = pl.pallas_call(
        functools.partial(_mm_cast_kernel, n_side=len(flat)),
        out_shape=[jax.ShapeDtypeStruct((M, N), out_dtype)] + side_shapes,
        grid=grid,
        in_specs=[pl.BlockSpec((bm, K), lambda j, i: (i, 0)),
                  pl.BlockSpec((None, K, bn), lambda j, i: (layer, 0, j))] + side_specs,
        out_specs=[pl.BlockSpec((bm, bn), lambda j, i: (i, j))] + side_specs,
        compiler_params=_params(("parallel", "parallel"), V7X_VMEM_BYTES - 4 * 1024 * 1024),
        name="matmul",
    )(a, w, *flat)
    return outs[0], [o.reshape(s.shape) for o, s in zip(outs[1:], side)]


def _mm_res_kernel(a_ref, w_ref, x_ref, g_ref, o_ref):
    o_ref[...] = x_ref[...] + g_ref[...] * _dot(a_ref[...], w_ref[...])


def _matmul_residual(a, w, layer, x2, mod3, k_gate, seq, *, blk):
    M, K = a.shape
    N = w.shape[2]
    bm = _tile(seq, blk, SUBLANE)
    bn = _tile(N, blk, LANE)
    nb = N // bn
    per_seq = seq // bm
    return pl.pallas_call(
        _mm_res_kernel,
        out_shape=jax.ShapeDtypeStruct((M, N), F32),
        grid=(nb, M // bm),
        in_specs=[
            pl.BlockSpec((bm, K), lambda j, i: (i, 0)),
            pl.BlockSpec((None, K, bn), lambda j, i: (layer, 0, j)),
            pl.BlockSpec((bm, bn), lambda j, i: (i, j)),
            pl.BlockSpec((None, 1, bn), lambda j, i: (i // per_seq, 0, k_gate * nb + j)),
        ],
        out_specs=pl.BlockSpec((bm, bn), lambda j, i: (i, j)),
        compiler_params=_params(("parallel", "parallel")),
        name="matmul_residual",
    )(a, w, x2, mod3)


def _out_norm_kernel(a_ref, w_ref, x_ref, g_ref, sh_ref, sc_ref, xo_ref, ho_ref, xs_ref, rs_ref):
    @pl.when(pl.program_id(0) == 0)
    def _():
        xs_ref[...] = jnp.zeros_like(xs_ref)
        rs_ref[...] = jnp.zeros_like(rs_ref)

    ho_ref[...] = (xs_ref[...] * rs_ref[...] * (1.0 + sc_ref[...]) + sh_ref[...]).astype(ho_ref.dtype)
    a = a_ref[...]
    n = xo_ref.shape[1]
    w = n // OUT_SLABS
    ss = None
    for k in range(OUT_SLABS):
        cs = slice(k * w, (k + 1) * w)
        xn = x_ref[:, cs] + g_ref[:, cs] * _dot(a, w_ref[:, cs])
        xo_ref[:, cs] = xn
        xs_ref[:, cs] = xn
        part = jnp.sum(xn * xn, axis=-1, keepdims=True)
        ss = part if ss is None else ss + part
    rs_ref[...] = lax.rsqrt(ss * (1.0 / n) + EPS)


def _out_proj_norm(a, w, layer, x2, mod3, k_gate, k_shift, k_scale, seq, *, bm):
    M, K = a.shape
    N = w.shape[2]
    per_seq = seq // bm
    n = M // bm

    def cur(i):
        return jnp.minimum(i, n - 1)

    def prev(i):
        return jnp.maximum(i - 1, 0)

    return pl.pallas_call(
        _out_norm_kernel,
        out_shape=(jax.ShapeDtypeStruct((M, N), F32), jax.ShapeDtypeStruct((M, N), BF16)),
        grid=(n + 1,),
        in_specs=[
            pl.BlockSpec((bm, K), lambda i: (cur(i), 0)),
            pl.BlockSpec((None, K, N), lambda i: (layer, 0, 0), pipeline_mode=pl.Buffered(1)),
            pl.BlockSpec((bm, N), lambda i: (cur(i), 0)),
            pl.BlockSpec((None, 1, N), lambda i: (cur(i) // per_seq, 0, k_gate)),
            pl.BlockSpec((None, 1, N), lambda i: (prev(i) // per_seq, 0, k_shift)),
            pl.BlockSpec((None, 1, N), lambda i: (prev(i) // per_seq, 0, k_scale)),
        ],
        out_specs=(pl.BlockSpec((bm, N), lambda i: (cur(i), 0)), pl.BlockSpec((bm, N), lambda i: (prev(i), 0))),
        scratch_shapes=[pltpu.VMEM((bm, N), F32), pltpu.VMEM((bm, 1), F32)],
        compiler_params=_params(("arbitrary",), V7X_VMEM_BYTES - 1024 * 1024),
        name="out_proj_norm",
    )(a, w, x2, mod3, mod3, mod3)


def _causal_conv3(p, halo, w):
    row = lax.broadcasted_iota(jnp.int32, p.shape, 0)
    s1 = pltpu.roll(p, 1, axis=0)
    s2 = pltpu.roll(p, 2, axis=0)
    h7 = halo[SUBLANE - 1:SUBLANE, :]
    h6 = halo[SUBLANE - 2:SUBLANE - 1, :]
    s1 = jnp.where(row == 0, h7, s1)
    s2 = jnp.where(row == 0, h6, jnp.where(row == 1, h7, s2))
    return w[0:1, :] * s2 + w[1:2, :] * s1 + w[2:3, :] * p


def _sconv_in_kernel(a_ref, wb_ref, wc_ref, wx_ref, cw_ref, o_ref, halo_ref, *, per_seq):
    i = pl.program_id(1)
    a = a_ref[...]
    p = _dot(a, wc_ref[...]) * _dot(a, wx_ref[...])
    halo = halo_ref[...]
    halo = jnp.where(i % per_seq == 0, jnp.zeros_like(halo), halo)
    halo_ref[...] = p[p.shape[0] - SUBLANE:, :]
    o_ref[...] = (_dot(a, wb_ref[...]) * _causal_conv3(p, halo, cw_ref[...])).astype(o_ref.dtype)


def _sconv_in(a, w_in, layer, conv_w3, seq):
    M, D = a.shape
    bm = _tile(seq, 1024, SUBLANE)
    bn = _tile(D, 512, LANE)
    nb = D // bn
    return pl.pallas_call(
        functools.partial(_sconv_in_kernel, per_seq=seq // bm),
        out_shape=jax.ShapeDtypeStruct((M, D), BF16),
        grid=(nb, M // bm),
        in_specs=[
            pl.BlockSpec((bm, D), lambda j, i: (i, 0)),
            pl.BlockSpec((None, D, bn), lambda j, i: (layer, 0, j)),
            pl.BlockSpec((None, D, bn), lambda j, i: (layer, 0, nb + j)),
            pl.BlockSpec((None, D, bn), lambda j, i: (layer, 0, 2 * nb + j)),
            pl.BlockSpec((3, bn), lambda j, i: (0, j)),
        ],
        out_specs=pl.BlockSpec((bm, bn), lambda j, i: (i, j)),
        scratch_shapes=[pltpu.VMEM((SUBLANE, bn), F32)],
        compiler_params=_params(("arbitrary", "arbitrary")),
        name="sconv_in",
    )(a, w_in, w_in, w_in, conv_w3)


def _ffn_up_kernel(a_ref, wa_ref, wu0_ref, wu1_ref, cw_ref, cb_ref, *refs, per_seq, ragged):
    if len(refs) == 2:
        (o_ref, halo_ref), job = refs, None
    else:
        ce_ref, aw_ref, ab_ref, o_ref, mo_ref, halo_ref = refs
        job = True
    j = pl.program_id(0)
    i = pl.program_id(1)
    bn = o_ref.shape[1]
    half = bn // 2

    def compute(width):
        if job:
            mo_ref[...] = _dot(ce_ref[...].astype(BF16), aw_ref[...].astype(BF16)) + ab_ref[...]
        a = a_ref[...]
        p = _dot(a, wa_ref[:, :width])
        halo = halo_ref[:, :width]
        halo = jnp.where(i % per_seq == 0, jnp.zeros_like(halo), halo)
        halo_ref[:, :width] = p[p.shape[0] - SUBLANE:, :]
        gate = _silu(_causal_conv3(p, halo, cw_ref[:, :width]) + cb_ref[:, :width])
        u = _dot(a, wu0_ref[...])
        if width == bn:
            u = jnp.concatenate([u, _dot(a, wu1_ref[...])], axis=1)
        o_ref[:, :width] = (gate * u).astype(o_ref.dtype)

    if ragged:
        last = pl.num_programs(0) - 1
        pl.when(j < last)(lambda: compute(bn))
        pl.when(j == last)(lambda: compute(half))
    else:
        compute(bn)


def _ffn_up(a, w_up, layer, conv_w3, conv_b, seq, mod_job=None):
    M, D = a.shape
    F = w_up.shape[2] // 2
    bm = _tile(seq, 1024, SUBLANE)
    bn = FFN_BN
    half = bn // 2
    assert F % half == 0
    nb = pl.cdiv(F, bn)
    u0 = F // half
    last = 2 * F // half - 1
    ni = M // bm
    in_specs = [
        pl.BlockSpec((bm, D), lambda j, i: (i, 0)),
        pl.BlockSpec((None, D, bn), lambda j, i: (layer, 0, j)),
        pl.BlockSpec((None, D, half), lambda j, i: (layer, 0, u0 + 2 * j)),
        pl.BlockSpec((None, D, half), lambda j, i: (layer, 0, jnp.minimum(u0 + 2 * j + 1, last))),
        pl.BlockSpec((3, bn), lambda j, i: (0, j)),
        pl.BlockSpec((1, bn), lambda j, i: (0, j)),
    ]
    out_shape = [jax.ShapeDtypeStruct((M, F), BF16)]
    out_specs = [pl.BlockSpec((bm, bn), lambda j, i: (i, j))]
    args = [a, w_up, w_up, w_up, conv_w3, conv_b]
    if mod_job is not None:
        c_emb, ada_w, ada_b, nxt = mod_job
        B, n_mod = c_emb.shape[0], ada_w.shape[2]
        n_tiles = n_mod // LANE
        if nb * ni < n_tiles:
            return (_ffn_up(a, w_up, layer, conv_w3, conv_b, seq),
                    _cond_matmul(c_emb, ada_w, ada_b, act=False, first=nxt, layers=1)[0])

        def tile_of(j, i):
            return jnp.minimum(j * ni + i, n_tiles - 1)

        in_specs += [pl.BlockSpec(c_emb.shape, lambda j, i: (0, 0)),
                     pl.BlockSpec((None, D, LANE), lambda j, i: (nxt, 0, tile_of(j, i))),
                     pl.BlockSpec((None, 1, LANE), lambda j, i: (nxt, 0, tile_of(j, i)))]
        out_shape.append(jax.ShapeDtypeStruct((B, n_mod), F32))
        out_specs.append(pl.BlockSpec((B, LANE), lambda j, i: (0, tile_of(j, i))))
        args += [c_emb, ada_w, ada_b.reshape(ada_b.shape[0], 1, n_mod)]
    outs = pl.pallas_call(
        functools.partial(_ffn_up_kernel, per_seq=seq // bm, ragged=F % bn != 0),
        out_shape=out_shape,
        grid=(nb, ni),
        in_specs=in_specs,
        out_specs=out_specs,
        scratch_shapes=[pltpu.VMEM((SUBLANE, bn), F32)],
        compiler_params=_params(("arbitrary", "arbitrary")),
        name="ffn_up",
    )(*args)
    return outs if mod_job is not None else outs[0]


DIAG = -2
N_FINE = 3
LOG_SPLIT = 2


def _hgrn_constants(C, W):
    u = np.arange(C)[None, :]
    t = np.arange(C)[:, None]
    bands = [u <= t]
    masks = []
    for l in range(1, N_FINE):
        h = 1 << l
        m = (t // (2 * h)) * (2 * h) + h
        upper = t >= m
        bands.append(np.where(upper, (u > m) & (u <= t), (u > t) & (u <= m)))
        masks.append(np.broadcast_to(upper[:SUBLANE], (SUBLANE, W)))
    d = np.concatenate(bands, axis=0).astype(np.float32)
    dmat = np.concatenate([d] * LOG_SPLIT, axis=1)
    rowmask = np.concatenate(masks, axis=0).astype(np.float32)
    s = np.arange(2 * C)[None, :] % C
    x = t ^ s
    lev = np.where(t > s, np.floor(np.log2(np.maximum(x, 1))), np.where(t == s, DIAG, -1))
    return jnp.asarray(dmat, BF16), jnp.asarray(rowmask, F32), jnp.asarray(lev, jnp.int32)


def _hgrn_kernel(q_ref, f_ref, v_ref, g_ref, lbl_ref, gn_ref, dmat_ref, rmask_ref, lev_ref,
                 *refs, n_chunks, slot, group, n_side):
    o_ref, st_ref = refs[n_side], refs[-1]
    _cast_blocks(refs[:n_side], refs[n_side + 1:-1])
    C = CHUNK
    W = 2 * HEAD
    n_lvl = int(np.log2(C))

    @pl.when(pl.program_id(2) == 0)
    def _():
        st_ref[...] = jnp.zeros_like(st_ref)

    rows = [lbl_ref[k:k + 1, :] for k in range(lbl_ref.shape[0])]
    mx = functools.reduce(jnp.maximum, rows)
    ex = [jnp.exp(r - mx) for r in rows]
    lb = functools.reduce(jnp.add, ex[:slot + 1]) / functools.reduce(jnp.add, ex)
    gn = gn_ref[...]
    zc = jnp.zeros((C, HEAD), BF16)
    zs = jnp.zeros((HEAD, HEAD), BF16)

    def by_head(x):
        return jnp.concatenate(
            [jnp.concatenate([x[:, :HEAD], zc], axis=1),
             jnp.concatenate([zc, x[:, HEAD:]], axis=1)], axis=0)

    G = group
    R = G * C

    def per_chunk(x):
        return [x[c * C:(c + 1) * C] for c in range(G)]

    def body(gi, carry):
        rs = pl.ds(pl.multiple_of(gi * R, R), R)
        q_all = q_ref[rs, :]
        f_all = lb + (1.0 - lb) * _sigmoid_tanh(f_ref[rs, :])
        k_all = 1.0 - f_all
        lg = jnp.log(f_all) * LOG2E
        pieces, rem = [], lg
        for _ in range(LOG_SPLIT - 1):
            pieces.append(rem.astype(BF16))
            rem = rem - pieces[-1].astype(F32)
        pieces.append(rem.astype(BF16))
        q, k, f = per_chunk(q_all), per_chunk(k_all), per_chunk(f_all)
        vb = per_chunk(v_ref[rs, :].astype(BF16))
        dmat = dmat_ref[...]
        args = [_dot(dmat, jnp.concatenate(list(xs), axis=0))
                for xs in zip(*[per_chunk(x) for x in pieces])]
        b = [x[0:C] for x in args]
        b_last = [x[C - 1:C, :] for x in b]
        e_fine = [jnp.exp2(x[C:]) for x in args]
        lev = lev_ref[...]

        a = [0.0] * G
        for l in range(1, N_FINE):
            up = jnp.tile(rmask_ref[(l - 1) * SUBLANE:l * SUBLANE, :], (C // SUBLANE, 1)) != 0.0
            for c in range(G):
                z = (jnp.where(up, q[c], k[c]) * e_fine[c][(l - 1) * C:l * C]).astype(BF16)
                a[c] = jnp.where(lev == l, _dot_nt(z, by_head(z)), a[c])
        lo = 1 << N_FINE
        qh = [[q[c][:lo] * jnp.exp2(b[c][:lo])] for c in range(G)]
        kh = [[k[c][C - lo:] * jnp.exp2(b_last[c] - b[c][C - lo:])] for c in range(G)]
        for l in range(N_FINE, n_lvl):
            h = 1 << l
            zh = jnp.zeros((h, W), BF16)
            for c in range(G):
                qe, ke = [], []
                for base in range(0, C, 2 * h):
                    m = base + h
                    bm = b[c][m:m + 1, :]
                    qp = q[c][m:m + h] * jnp.exp2(b[c][m:m + h] - bm)
                    kp = k[c][base:m] * jnp.exp2(bm - b[c][base:m])
                    if base == 0:
                        qh[c].append(qp * jnp.exp2(bm))
                    if base == C - 2 * h:
                        kh[c].insert(0, kp * jnp.exp2(b_last[c] - bm))
                    qe += [zh, qp.astype(BF16)]
                    ke += [kp.astype(BF16), zh]
                p = _dot_nt(jnp.concatenate(qe, axis=0), by_head(jnp.concatenate(ke, axis=0)))
                a[c] = jnp.where(lev == l, p, a[c])
        qh = [jnp.concatenate(x, axis=0).astype(BF16) for x in qh]
        kh = [jnp.concatenate(x, axis=0).astype(BF16) for x in kh]

        def head_sums(x):
            return jnp.concatenate(
                [jnp.broadcast_to(jnp.sum(x[:, :HEAD], axis=-1, keepdims=True), (C, C)),
                 jnp.broadcast_to(jnp.sum(x[:, HEAD:], axis=-1, keepdims=True), (C, C))], axis=1)

        for c in range(G):
            a[c] = jnp.where(lev == DIAG, head_sums(q[c] * k[c]), a[c])
            a[c] = jnp.where(lev == 0, head_sums(q[c] * f[c] * pltpu.roll(k[c], 1, axis=0)), a[c])

        upd0 = [_dot_tn(vb[c][:, :HEAD], kh[c][:, :HEAD]) for c in range(G)]
        upd1 = [_dot_tn(vb[c][:, HEAD:], kh[c][:, HEAD:]) for c in range(G)]
        o_intra = [_dot(a[c].astype(BF16), by_head(vb[c])) for c in range(G)]
        d_s = [jnp.exp2(x) for x in b_last]
        s0 = st_ref[0]
        s1 = st_ref[1]
        o = []
        for c in range(G):
            s_bd = jnp.concatenate(
                [jnp.concatenate([s0.astype(BF16), zs], axis=1),
                 jnp.concatenate([zs, s1.astype(BF16)], axis=1)], axis=0)
            o.append(_dot_nt(qh[c], s_bd) + o_intra[c])
            s0 = s0 * d_s[c][:, :HEAD] + upd0[c]
            s1 = s1 * d_s[c][:, HEAD:] + upd1[c]
        st_ref[0] = s0
        st_ref[1] = s1

        o_all = jnp.concatenate(o, axis=0)
        sq = o_all * o_all
        ms = jnp.concatenate(
            [jnp.broadcast_to(jnp.mean(sq[:, :HEAD], axis=-1, keepdims=True), (R, HEAD)),
             jnp.broadcast_to(jnp.mean(sq[:, HEAD:], axis=-1, keepdims=True), (R, HEAD))], axis=1)
        g = g_ref[rs, :]
        o_ref[rs, :] = (o_all * lax.rsqrt(ms + EPS) * gn * (g * _sigmoid_tanh(g))).astype(o_ref.dtype)
        return carry

    lax.fori_loop(0, n_chunks // G, body, 0)


def _hgrn_recurrence(proj, lb_logits, g_norm, batch, seq, slot, side=()):
    M, D4 = proj.shape
    D = D4 // 4
    W = 2 * HEAD
    npair = D // W
    tb = _tile(seq, 2048, CHUNK)
    nt = seq // tb
    group = _tile(tb // CHUNK, 8, 1)
    consts = _hgrn_constants(CHUNK, W)

    def col(part):
        return pl.BlockSpec((tb, W), lambda b, p, t: (b * nt + t, part * npair + p))

    def whole(arr):
        return pl.BlockSpec(arr.shape, lambda b, p, t: (0, 0))

    grid = (batch, npair, nt)
    casts = _side_casts(side, grid)
    if casts is None:
        return (_hgrn_recurrence(proj, lb_logits, g_norm, batch, seq, slot)[0],
                [s.astype(BF16) for s in side])
    flat, side_specs, side_shapes = casts
    outs = pl.pallas_call(
        functools.partial(_hgrn_kernel, n_chunks=tb // CHUNK, slot=slot, group=group,
                          n_side=len(flat)),
        out_shape=[jax.ShapeDtypeStruct((M, D), BF16)] + side_shapes,
        grid=grid,
        in_specs=[col(0), col(1), col(2), col(3),
                  pl.BlockSpec((lb_logits.shape[0], W), lambda b, p, t: (0, p)),
                  pl.BlockSpec((1, W), lambda b, p, t: (0, p))]
        + [whole(c) for c in consts] + side_specs,
        out_specs=[pl.BlockSpec((tb, W), lambda b, p, t: (b * nt + t, p))] + side_specs,
        scratch_shapes=[pltpu.VMEM((2, HEAD, HEAD), F32)],
        compiler_params=_params(("parallel", "parallel", "arbitrary")),
        name="hgrn_recurrence",
    )(proj, proj, proj, proj, lb_logits, g_norm.reshape(1, D), *consts, *flat)
    return outs[0], [o.reshape(s.shape) for o, s in zip(outs[1:], side)]


def kernel(x, c, w_cond, b_cond, ada_w, ada_b, lb_logits, hgrn_w_in, hgrn_norm, hgrn_w_out,
           sconv_w_in, sconv_conv_w, sconv_w_out, ffn_w_up, ffn_conv_w, ffn_conv_b,
           ffn_w_down, final_norm):
    B, T, D = x.shape
    depth = ada_w.shape[0]
    F = ffn_conv_b.shape[1]
    M = B * T
    n_mixers = 2

    c_emb = _cond_matmul(c, w_cond[None], b_cond[None], act=True)[0]
    mod_i = _cond_matmul(c_emb, ada_w, ada_b, act=False, layers=1)[0]

    w_hin = hgrn_w_in.astype(BF16)

    x2 = x.reshape(M, D)
    for i in range(depth):
        mod3 = mod_i.reshape(B, 1, N_MOD * D)
        j = i // n_mixers
        hm = _prenorm(x2.reshape(B, T, D), mod3, 0, 1).reshape(M, D)
        if i % n_mixers == 0:
            first = i == 0
            proj, copies = _matmul(hm, w_hin, j, F32, side=(hgrn_w_out, ffn_w_up) if first else ())
            if first:
                w_hout, w_up = copies
            y, copies = _hgrn_recurrence(proj, lb_logits, hgrn_norm[j], B, T, slot=i,
                                         side=(sconv_w_in, sconv_w_out, ffn_w_down) if first else ())
            if first:
                w_sin, w_sout, w_down = copies
            x2, hf = _out_proj_norm(y, w_hout, j, x2, mod3, 2, 3, 4, T, bm=OUT_BM)
        else:
            y = _sconv_in(hm, w_sin, j, sconv_conv_w[j].T, T)
            x2, hf = _out_proj_norm(y, w_sout, j, x2, mod3, 2, 3, 4, T, bm=OUT_BM)

        if i + 1 < depth:
            h, mod_i = _ffn_up(hf, w_up, i, ffn_conv_w[i].T, ffn_conv_b[i].reshape(1, F), T,
                               mod_job=(c_emb, ada_w, ada_b, i + 1))
        else:
            h = _ffn_up(hf, w_up, i, ffn_conv_w[i].T, ffn_conv_b[i].reshape(1, F), T)
        x2 = _matmul_residual(h, w_down, i, x2, mod3, 5, T, blk=512)
    return _final_norm(x2, final_norm).reshape(B, T, D)
```
